```python
import jax, jax.numpy as jnp
from jax import lax
import numpy as np

D_MODEL = 2048
BATCH = 4
SEQ = 4096
DEPTH = 4

MIX_WIDTH = D_MODEL
POOL_WIDTH = MIX_WIDTH // 2
RWKV_WIDTH = MIX_WIDTH - POOL_WIDTH
POOL_WINDOWS = (2, 4, 8, 16)
N_POOL_GROUPS = len(POOL_WINDOWS)
POOL_GROUP = POOL_WIDTH // N_POOL_GROUPS
HEAD_SIZE = 64
N_RWKV_HEADS = RWKV_WIDTH // HEAD_SIZE
D_DECAY_LORA = 64
D_AAA_LORA = 64
D_MV_LORA = 32
D_GATE_LORA = 160
D_FF = 4 * D_MODEL
D_PLE = 256
NORM_EPS = 1e-6
GN_EPS = 64e-5
SHIFT_WIDTH = 3 * RWKV_WIDTH + D_DECAY_LORA + D_AAA_LORA + D_GATE_LORA
IN_WIDTH = POOL_WIDTH + SHIFT_WIDTH

kernel_name = "hymba_pool_rwkv7_hybrid"


def rms_norm(x, g):
    xf = x.astype(jnp.float32)
    y = xf * lax.rsqrt(jnp.mean(xf * xf, axis=-1, keepdims=True) + NORM_EPS)
    return (y * g.astype(jnp.float32)).astype(x.dtype)


def token_shift(z, mu):
    zf = z.astype(jnp.float32)
    z_prev = jnp.pad(zf, ((0, 0), (1, 0), (0, 0)))[:, :-1]
    return zf + (z_prev - zf) * mu.astype(jnp.float32)


def causal_multiscale_pool(u):
    B, T, _ = u.shape
    uf = u.astype(jnp.float32).reshape(B, T, N_POOL_GROUPS, POOL_GROUP)
    cs = jnp.cumsum(uf, axis=1)
    t = jnp.arange(T)
    outs = []
    for gi, win in enumerate(POOL_WINDOWS):
        c = cs[:, :, gi]
        lagged = jnp.pad(c, ((0, 0), (win, 0), (0, 0)))[:, :T]
        cnt = jnp.minimum(t + 1, win).astype(jnp.float32)[None, :, None]
        outs.append((c - lagged) / cnt - uf[:, :, gi])
    return jnp.stack(outs, axis=2)


def wkv7_scan(r, decay, k, v, a_vec, b_vec):
    B, T, H, N = r.shape
    xs = tuple(jnp.swapaxes(z, 0, 1) for z in (r, decay, k, v, a_vec, b_vec))

    def step(S, inp):
        r_t, w_t, k_t, v_t, a_t, b_t = inp
        sa = jnp.einsum('bhij,bhj->bhi', S, a_t)
        S = S * w_t[:, :, None, :] + sa[..., None] * b_t[:, :, None, :] + v_t[..., None] * k_t[:, :, None, :]
        y = jnp.einsum('bhij,bhj->bhi', S, r_t)
        return S, y

    S0 = jnp.zeros((B, H, N, N), jnp.float32)
    _, y = lax.scan(step, S0, xs)
    return jnp.swapaxes(y, 0, 1)


def rwkv7_time_mix(zr, zk, zv, zw, za, zg, zvr, v_first, w0, w_up, a0, a_up, g_up,
                   v0, v_up, k_k, k_a, r_k, gn_g, gn_b):
    f32 = jnp.float32
    B, T, _ = zr.shape
    H, N = N_RWKV_HEADS, HEAD_SIZE
    r = zr
    k = zk
    v = zv
    w = -jax.nn.softplus(-(w0.astype(f32) + jnp.tanh(zw) @ w_up.astype(f32))) - 0.5
    decay = jnp.exp(-jnp.exp(w))
    a = jax.nn.sigmoid(a0.astype(f32) + za @ a_up.astype(f32))
    g = jax.nn.sigmoid(zg) @ g_up.astype(f32)
    if zvr is None:
        v_first = v
    else:
        v = v + (v_first - v) * jax.nn.sigmoid(v0.astype(f32) + zvr @ v_up.astype(f32))
    hs = lambda z: z.reshape(B, T, H, N)
    kk = hs(k * k_k.astype(f32))
    kk = kk / jnp.maximum(jnp.linalg.norm(kk, axis=-1, keepdims=True), 1e-12)
    k = k * (1.0 + (a - 1.0) * k_a.astype(f32))
    r_h, k_h, v_h, a_h = hs(r), hs(k), hs(v), hs(a)
    y = wkv7_scan(r_h, hs(decay), k_h, v_h, -kk, kk * a_h)
    mu = jnp.mean(y, axis=-1, keepdims=True)
    var = jnp.mean(jnp.square(y - mu), axis=-1, keepdims=True)
    y = (y - mu) * lax.rsqrt(var + GN_EPS) * gn_g.astype(f32).reshape(H, N) + gn_b.astype(f32).reshape(H, N)
    y = y + jnp.sum(r_h * k_h * r_k.astype(f32), axis=-1, keepdims=True) * v_h
    return y.reshape(B, T, RWKV_WIDTH) * g, v_first


def setup_inputs(seed: int = 0) -> dict:
    key = jax.random.key(seed)
    ks = iter(jax.random.split(key, 40))
    f32 = jnp.float32

    def nrm(shape, scale):
        return jax.random.normal(next(ks), shape, f32) * scale

    def unif(shape, lo, hi):
        return jax.random.uniform(next(ks), shape, f32, minval=lo, maxval=hi)

    L, Lm1, D, R = DEPTH, DEPTH - 1, D_MODEL, RWKV_WIDTH
    return {
        "x": nrm((BATCH, SEQ, D), 1.0),
        "p": nrm((DEPTH, BATCH, SEQ, D_PLE), 1.0),
        "attn_norm": 1.0 + nrm((L, D), 0.02),
        "w_in": nrm((L, D, IN_WIDTH), D ** -0.5),
        "mu_shift": unif((L, SHIFT_WIDTH), 0.0, 1.0),
        "w_vres_dn": nrm((Lm1, D, D_MV_LORA), D ** -0.5),
        "mu_vres": unif((Lm1, D_MV_LORA), 0.0, 1.0),
        "v0": nrm((Lm1, R), 0.5),
        "v_up": nrm((Lm1, D_MV_LORA, R), D_MV_LORA ** -0.5),
        "pool_w": nrm((L, N_POOL_GROUPS, POOL_GROUP, POOL_GROUP), POOL_GROUP ** -0.5),
        "pool_scale": 0.5 + nrm((L, POOL_WIDTH), 0.1),
        "w0": unif((L, R), -4.0, 1.0),
        "w_up": nrm((L, D_DECAY_LORA, R), D_DECAY_LORA ** -0.5),
        "a0": nrm((L, R), 0.3),
        "a_up": nrm((L, D_AAA_LORA, R), D_AAA_LORA ** -0.5),
        "g_up": nrm((L, D_GATE_LORA, R), D_GATE_LORA ** -0.5),
        "k_k": 0.85 + nrm((L, R), 0.05),
        "k_a": 1.0 + nrm((L, R), 0.05),
        "r_k": nrm((L, N_RWKV_HEADS, HEAD_SIZE), 0.1),
        "gn_g": 1.0 + nrm((L, R), 0.02),
        "gn_b": nrm((L, R), 0.02),
        "w_out": nrm((L, MIX_WIDTH, D), MIX_WIDTH ** -0.5),
        "mlp_norm": 1.0 + nrm((L, D), 0.02),
        "w_ffn_up": nrm((L, D, D_FF), D ** -0.5),
        "w_ffn_down": nrm((L, D_FF, D), D_FF ** -0.5),
        "ple_norm": 1.0 + nrm((L, D), 0.02),
        "w_ple_gate": nrm((L, D, D), D ** -0.5),
        "w_ple_proj": nrm((L, D_PLE, D), D_PLE ** -0.5),
        "final_norm": 1.0 + nrm((D,), 0.02),
    }


def reference(x, p, attn_norm, w_in, mu_shift, w_vres_dn, mu_vres, v0, v_up, pool_w, pool_scale,
              w0, w_up, a0, a_up, g_up, k_k, k_a, r_k, gn_g, gn_b, w_out, mlp_norm, w_ffn_up,
              w_ffn_down, ple_norm, w_ple_gate, w_ple_proj, final_norm):
    B, T, _ = x.shape
    R = RWKV_WIDTH
    o_w = 3 * R
    o_a = o_w + D_DECAY_LORA
    o_g = o_a + D_AAA_LORA
    v_first = None
    for i in range(DEPTH):
        h = rms_norm(x, attn_norm[i])
        if i == 0:
            z = h @ w_in[0]
        else:
            z = h @ jnp.concatenate([w_in[i], w_vres_dn[i - 1]], axis=1)
        z_pool = z[..., :POOL_WIDTH]
        zs = token_shift(z[..., POOL_WIDTH:IN_WIDTH], mu_shift[i])
        zvr = None if i == 0 else token_shift(z[..., IN_WIDTH:], mu_vres[i - 1])

        d = causal_multiscale_pool(z_pool)
        pool_out = jnp.einsum('btgc,gcd->btgd', d, pool_w[i].astype(jnp.float32)).reshape(B, T, POOL_WIDTH)
        pool_out = pool_out * pool_scale[i].astype(jnp.float32)

        rwkv_out, v_first = rwkv7_time_mix(
            zs[..., :R], zs[..., R:2 * R], zs[..., 2 * R:o_w], zs[..., o_w:o_a], zs[..., o_a:o_g], zs[..., o_g:],
            zvr, v_first, w0[i], w_up[i], a0[i], a_up[i], g_up[i],
            None if i == 0 else v0[i - 1], None if i == 0 else v_up[i - 1],
            k_k[i], k_a[i], r_k[i], gn_g[i], gn_b[i])

        mix = jnp.concatenate([pool_out, rwkv_out], axis=-1).astype(x.dtype)
        x = x + mix @ w_out[i]

        h2 = rms_norm(x, mlp_norm[i])
        x = x + jnp.square(jax.nn.relu(h2 @ w_ffn_up[i])) @ w_ffn_down[i]

        gate = jax.nn.sigmoid(rms_norm(x, ple_norm[i]) @ w_ple_gate[i])
        x = x + gate * (p[i] @ w_ple_proj[i])
    return rms_norm(x, final_norm)
```

```python
import functools

import jax
import jax.numpy as jnp
from jax import lax
from jax.experimental import pallas as pl
from jax.experimental.pallas import tpu as pltpu

F32 = jnp.float32
BF16 = jnp.bfloat16

NORM_EPS = 1e-6
GN_EPS = 64e-5
HEAD_SIZE = 64
POOL_WINDOWS = (2, 4, 8, 16)
POOL_HALO = 16
SHIFT_HALO = 8
WKV_CHUNK = 64
WKV_HEADS_PER_GROUP = 2
SEG_SLAB = 256
LANE = 128
VMEM_LIMIT = 52 * 1024 * 1024


def _dot(a, b):
    return jnp.dot(a, b, preferred_element_type=F32)


def _dot_nt(a, b):
    return lax.dot_general(a, b, (((1,), (1,)), ((), ())), preferred_element_type=F32)


def _dot_tn(a, b):
    return lax.dot_general(a, b, (((0,), (0,)), ((), ())), preferred_element_type=F32)


def _params(*sem):
    return pltpu.CompilerParams(dimension_semantics=sem, vmem_limit_bytes=VMEM_LIMIT)


def _rms_rows(x, g):
    ms = jnp.mean(x * x, axis=-1, keepdims=True)
    return x * lax.rsqrt(ms + NORM_EPS) * g


def _norm_into(h_ref, x_ref, g_ref, rows=256):
    rows = min(rows, x_ref.shape[0])
    n = x_ref.shape[0] // rows

    def body(c, carry):
        rs = pl.ds(pl.multiple_of(c * rows, rows), rows)
        h_ref[rs, :] = _rms_rows(x_ref[rs, :], g_ref[...]).astype(h_ref.dtype)
        return carry

    lax.fori_loop(0, n, body, 0)


def _segsum(x, e):
    outs = []
    for s in range(x.shape[1] // SEG_SLAB):
        xs = x[:, s * SEG_SLAB:(s + 1) * SEG_SLAB]
        hi = xs.astype(BF16)
        lo = (xs - hi.astype(F32)).astype(BF16)
        outs.append(_dot(hi, e) + _dot(lo, e))
    return jnp.concatenate(outs, axis=1)


def _rms_matmul_kernel(x_ref, g_ref, w_ref, o_ref, h_ref):
    @pl.when(pl.program_id(1) == 0)
    def _():
        _norm_into(h_ref, x_ref, g_ref)

    o_ref[...] = _dot(h_ref[...], w_ref[...])


def _rms_matmul(x, g, w, layer, tm, tn):
    m, d = x.shape
    n = w.shape[2]
    tm, tn = min(tm, m), min(tn, n)
    return pl.pallas_call(
        _rms_matmul_kernel,
        grid=(m // tm, n // tn),
        in_specs=[
            pl.BlockSpec((tm, d), lambda i, j: (i, 0)),
            pl.BlockSpec((None, 1, d), lambda i, j: (layer, 0, 0)),
            pl.BlockSpec((None, d, tn), lambda i, j: (layer, 0, j)),
        ],
        out_specs=pl.BlockSpec((tm, tn), lambda i, j: (i, j)),
        out_shape=jax.ShapeDtypeStruct((m, n), F32),
        scratch_shapes=[pltpu.VMEM((tm, d), BF16)],
        compiler_params=_params("parallel", "arbitrary"),
        name="rms_matmul",
    )(x, g, w)


def _outproj_kernel(x_ref, ma_ref, mb_ref, wa_ref, wb_ref, o_ref):
    o_ref[...] = x_ref[...] + _dot(ma_ref[...], wa_ref[...]) + _dot(mb_ref[...], wb_ref[...])


def _outproj(x, mix_a, mix_b, w, layer, tm, tn):
    m, d = x.shape
    ka, kb = mix_a.shape[1], mix_b.shape[1]
    tm, tn = min(tm, m), min(tn, d)
    assert ka == kb
    return pl.pallas_call(
        _outproj_kernel,
        grid=(m // tm, d // tn),
        in_specs=[
            pl.BlockSpec((tm, tn), lambda i, j: (i, j)),
            pl.BlockSpec((tm, ka), lambda i, j: (i, 0)),
            pl.BlockSpec((tm, kb), lambda i, j: (i, 0)),
            pl.BlockSpec((None, None, ka, tn), lambda i, j: (layer, 0, 0, j)),
            pl.BlockSpec((None, None, kb, tn), lambda i, j: (layer, 1, 0, j)),
        ],
        out_specs=pl.BlockSpec((tm, tn), lambda i, j: (i, j)),
        out_shape=jax.ShapeDtypeStruct((m, d), F32),
        compiler_params=_params("parallel", "parallel"),
        name="outproj",
    )(x, mix_a, mix_b, w, w)


def _ffn_kernel(x_ref, g_ref, wu_ref, wd_ref, o_ref, h_ref):
    @pl.when(pl.program_id(1) == 0)
    def _():
        _norm_into(h_ref, x_ref, g_ref)
        o_ref[...] = x_ref[...]

    a = _dot(h_ref[...], wu_ref[...])
    a = jnp.square(jnp.maximum(a, 0.0)).astype(BF16)
    o_ref[...] += _dot(a, wd_ref[...])


def _ffn(x, g, w_up, w_down, layer, tm, tf):
    m, d = x.shape
    f = w_up.shape[2]
    tm, tf = min(tm, m), min(tf, f)
    return pl.pallas_call(
        _ffn_kernel,
        grid=(m // tm, f // tf),
        in_specs=[
            pl.BlockSpec((tm, d), lambda i, j: (i, 0)),
            pl.BlockSpec((None, 1, d), lambda i, j: (layer, 0, 0)),
            pl.BlockSpec((None, d, tf), lambda i, j: (layer, 0, j)),
            pl.BlockSpec((None, tf, d), lambda i, j: (layer, j, 0)),
        ],
        out_specs=pl.BlockSpec((tm, d), lambda i, j: (i, 0)),
        out_shape=jax.ShapeDtypeStruct((m, d), F32),
        scratch_shapes=[pltpu.VMEM((tm, d), BF16)],
        compiler_params=_params("parallel", "arbitrary"),
        name="ffn",
    )(x, g, w_up, w_down)


def _ple_kernel(final, tn, x_ref, g_ref, wg_ref, p_ref, wp_ref, fg_ref, o_ref, h_ref):
    _norm_into(h_ref, x_ref, g_ref)
    pb = p_ref[...].astype(BF16)
    for c in range(x_ref.shape[1] // tn):
        cs = slice(c * tn, (c + 1) * tn)
        gate = jax.nn.sigmoid(_dot(h_ref[...], wg_ref[:, cs]))
        o_ref[:, cs] = x_ref[:, cs] + gate * _dot(pb, wp_ref[:, cs])
    if final:
        _norm_into(o_ref, o_ref, fg_ref)


def _ple(x, g, w_gate, p, w_proj, final_g, layer, final, tm):
    m, d = x.shape
    dp = p.shape[1]
    tm = min(tm, m)
    nblk = m // tm
    return pl.pallas_call(
        functools.partial(_ple_kernel, final, min(512, d)),
        grid=(nblk,),
        in_specs=[
            pl.BlockSpec((tm, d), lambda i: (i, 0)),
            pl.BlockSpec((None, 1, d), lambda i: (layer, 0, 0)),
            pl.BlockSpec((None, d, d), lambda i: (layer, 0, 0)),
            pl.BlockSpec((tm, dp), lambda i: (layer * nblk + i, 0)),
            pl.BlockSpec((None, dp, d), lambda i: (layer, 0, 0)),
            pl.BlockSpec((1, d), lambda i: (0, 0)),
        ],
        out_specs=pl.BlockSpec((tm, d), lambda i: (i, 0)),
        out_shape=jax.ShapeDtypeStruct((m, d), F32),
        scratch_shapes=[pltpu.VMEM((tm, d), BF16)],
        compiler_params=_params("parallel"),
        name="ple",
    )(x, g, w_gate, p, w_proj, final_g)


def _pool_kernel(z_ref, halo_ref, pw_ref, ps_ref, o_ref):
    i = pl.program_id(1)
    tt, width = z_ref.shape[1], z_ref.shape[2]
    pg = width // len(POOL_WINDOWS)
    u = z_ref[0]
    halo = jnp.where(i == 0, 0.0, halo_ref[0])
    cur = jnp.concatenate([halo, u], axis=0)
    span, lane0 = 1, 0
    t = i * tt + lax.broadcasted_iota(jnp.int32, (tt, pg), 0)
    for gi, win in enumerate(POOL_WINDOWS):
        cur = cur[:, gi * pg - lane0:]
        lane0 = gi * pg
        while span < win:
            cur = cur + pltpu.roll(cur, span, 0)
            span *= 2
        cols = slice(gi * pg, (gi + 1) * pg)
        cnt = jnp.minimum(t + 1, win).astype(F32)
        d = cur[POOL_HALO:, :pg] / cnt - u[:, cols]
        o = _dot(d.astype(BF16), pw_ref[gi]) * ps_ref[:, cols]
        o_ref[0, :, cols] = o.astype(o_ref.dtype)


def _pool(z, pool_w, pool_scale, layer, width, tt):
    b, t, _ = z.shape
    g, pg, _ = pool_w.shape[1:]
    tt = min(tt, t)
    assert all(w & (w - 1) == 0 for w in POOL_WINDOWS) and max(POOL_WINDOWS) <= POOL_HALO
    assert list(POOL_WINDOWS) == sorted(POOL_WINDOWS) and g == len(POOL_WINDOWS) and g * pg == width
    hb = tt // POOL_HALO
    return pl.pallas_call(
        _pool_kernel,
        grid=(b, t // tt),
        in_specs=[
            pl.BlockSpec((1, tt, width), lambda bi, i: (bi, i, 0)),
            pl.BlockSpec((1, POOL_HALO, width), lambda bi, i: (bi, jnp.maximum(i * hb - 1, 0), 0)),
            pl.BlockSpec((None, g, pg, pg), lambda bi, i: (layer, 0, 0, 0)),
            pl.BlockSpec((None, 1, width), lambda bi, i: (layer, 0, 0)),
        ],
        out_specs=pl.BlockSpec((1, tt, width), lambda bi, i: (bi, i, 0)),
        out_shape=jax.ShapeDtypeStruct((b, t, width), BF16),
        compiler_params=_params("parallel", "parallel"),
        name="pool",
    )(z, z, pool_w, pool_scale)


def _prep_kernel(first, dims, zr_ref, zk_ref, zv_ref, zl_ref, hr_ref, hk_ref, hv_ref, hl_ref,
                 mu_ref, vec_ref, w1_ref, w2_ref, e_ref, vf_ref,
                 r_o, lw_o, k_o, v_o, a_o, b_o, g_o):
    width, d_w, d_g = dims
    i = pl.program_id(1)

    def shifted(z_ref, h_ref, mu):
        z = z_ref[0]
        prev = jnp.where(i == 0, 0.0, h_ref[0, SHIFT_HALO - 1:SHIFT_HALO, :])
        zp = pltpu.roll(z, 1, 0)
        row = lax.broadcasted_iota(jnp.int32, z.shape, 0)
        zp = jnp.where(row == 0, prev, zp)
        return z + (zp - z) * mu

    w0, a0, v0 = vec_ref[0:1, :], vec_ref[1:2, :], vec_ref[2:3, :]
    k_k, k_a = vec_ref[3:4, :], vec_ref[4:5, :]

    lz = shifted(zl_ref, hl_ref, mu_ref[:, 3 * width:])
    s1 = lz[:, :LANE]
    lane1 = lax.broadcasted_iota(jnp.int32, s1.shape, 1)
    act1 = jnp.where(lane1 < d_w, jnp.tanh(s1), s1)
    lo1 = _dot(act1.astype(BF16), w1_ref[...])
    s2 = lz[:, LANE:LANE + w2_ref.shape[0]]
    lane2 = lax.broadcasted_iota(jnp.int32, s2.shape, 1)
    act2 = jnp.where(lane2 < d_g, jax.nn.sigmoid(s2), s2)
    lo2 = _dot(act2.astype(BF16), w2_ref[...])

    u = w0 + lo1[:, :width]
    w = -(jnp.maximum(-u, 0.0) + jnp.log1p(jnp.exp(-jnp.abs(u)))) - 0.5
    lw_o[0] = -jnp.exp(w)
    a = jax.nn.sigmoid(a0 + lo1[:, width:])
    g_o[0] = lo2[:, :width]

    r_o[0] = shifted(zr_ref, hr_ref, mu_ref[:, :width])
    k = shifted(zk_ref, hk_ref, mu_ref[:, width:2 * width])
    v = shifted(zv_ref, hv_ref, mu_ref[:, 2 * width:3 * width])
    if not first:
        v = v + (vf_ref[0] - v) * jax.nn.sigmoid(v0 + lo2[:, width:])
    v_o[0] = v

    x = k * k_k
    kk = x / jnp.maximum(jnp.sqrt(_segsum(x * x, e_ref[...])), 1e-12)
    k_o[0] = k * (1.0 + (a - 1.0) * k_a)
    a_o[0] = -kk
    b_o[0] = kk * a


def _prep(z, mu, vecs, w1, w2, e, v_first, layer, first, width, d_w, d_g, tt):
    b, t, zw = z.shape
    tt = min(tt, t)
    lw = zw - 4 * width
    hb = tt // SHIFT_HALO
    lcol = 4 * width // lw
    assert 4 * width % lw == 0

    def main(col, w):
        return pl.BlockSpec((1, tt, w), lambda bi, i: (bi, i, col))

    def halo(col, w):
        return pl.BlockSpec((1, SHIFT_HALO, w), lambda bi, i: (bi, jnp.maximum(i * hb - 1, 0), col))

    def whole(a):
        return pl.BlockSpec((None,) + a.shape[1:], lambda bi, i: (layer,) + (0,) * (a.ndim - 1))

    out = jax.ShapeDtypeStruct((b, t, width), F32)
    return pl.pallas_call(
        functools.partial(_prep_kernel, first, (width, d_w, d_g)),
        grid=(b, t // tt),
        in_specs=[
            main(1, width), main(2, width), main(3, width), main(lcol, lw),
            halo(1, width), halo(2, width), halo(3, width), halo(lcol, lw),
            whole(mu), whole(vecs), whole(w1), whole(w2),
            pl.BlockSpec(e.shape, lambda bi, i: (0, 0)),
            pl.BlockSpec((1, tt, width), lambda bi, i: (bi, i, 0)),
        ],
        out_specs=[pl.BlockSpec((1, tt, width), lambda bi, i: (bi, i, 0))] * 7,
        out_shape=[out] * 7,
        compiler_params=_params("parallel", "parallel"),
        name="rwkv_prep",
    )(z, z, z, z, z, z, z, z, mu, vecs, w1, w2, e, v_first)


def _wkv_kernel(r_ref, lw_ref, k_ref, v_ref, a_ref, b_ref, y_ref, s_ref):
    hg = WKV_HEADS_PER_GROUP
    c = pl.program_id(1)
    L, C = r_ref.shape[1], r_ref.shape[2]
    W, GL = hg * HEAD_SIZE, hg * L
    ng = C // W

    @pl.when(c == 0)
    def _():
        s_ref[...] = jnp.zeros_like(s_ref)

    lw = lw_ref[0]
    ti = lax.broadcasted_iota(jnp.int32, (L, L), 0)
    tj = lax.broadcasted_iota(jnp.int32, (L, L), 1)
    tri = jnp.where(tj <= ti, 1.0, 0.0).astype(BF16)
    h1 = lw.astype(BF16)
    r1 = lw - h1.astype(F32)
    h2 = r1.astype(BF16)
    h3 = (r1 - h2.astype(F32)).astype(BF16)
    cum = _dot(tri, h1) + _dot(tri, h2) + _dot(tri, h3)
    tot = cum[L - 1:L, :]
    e_inv = jnp.exp(-cum)
    e_end = jnp.exp(tot - cum)
    wl = jnp.exp(tot)
    at = a_ref[0] * jnp.exp(cum - lw)
    rt = r_ref[0] * jnp.exp(cum)
    bt = b_ref[0] * e_inv
    kt = k_ref[0] * e_inv
    bp = (b_ref[0] * e_end).astype(BF16)
    kp = (k_ref[0] * e_end).astype(BF16)
    vv = v_ref[0]

    lane = lax.broadcasted_iota(jnp.int32, (L, W), 1)
    head_mask = [(lane >= h * HEAD_SIZE) & (lane < (h + 1) * HEAD_SIZE) for h in range(hg)]
    row2 = lax.broadcasted_iota(jnp.int32, (L, 2 * GL), 0)
    col2 = lax.broadcasted_iota(jnp.int32, (L, 2 * GL), 1) & (L - 1)
    strict = col2 < row2
    incl = col2 <= row2
    colg = lax.broadcasted_iota(jnp.int32, (L, GL), 1)
    rowg = lax.broadcasted_iota(jnp.int32, (L, GL), 0)
    eye = (colg & (L - 1)) == rowg
    blk_mask = [(colg >= h * L) & (colg < (h + 1) * L) for h in range(hg)]
    sr = lax.broadcasted_iota(jnp.int32, (W, W), 0)
    sc = lax.broadcasted_iota(jnp.int32, (W, W), 1)
    same_head = (sr // HEAD_SIZE) == (sc // HEAD_SIZE)

    def stack_heads(x):
        return jnp.concatenate([jnp.where(m, x, 0.0).astype(BF16) for m in head_mask], axis=0)

    def block_diag(q):
        return jnp.concatenate([jnp.where(m, q, 0.0).astype(BF16) for m in blk_mask], axis=0)

    for g in range(ng):
        sl = slice(g * W, (g + 1) * W)
        at_g, rt_g = at[:, sl].astype(BF16), rt[:, sl].astype(BF16)
        lhs = jnp.concatenate([at_g, rt_g], axis=0)
        rhs = jnp.concatenate([stack_heads(bt[:, sl]), stack_heads(kt[:, sl])], axis=0)
        gm = _dot_nt(lhs, rhs)
        pa = jnp.where(strict, gm[:L], 0.0)
        pr = jnp.where(incl, gm[L:], 0.0)

        a_mat = pa[:, :GL]
        tm = jnp.where(eye, 1.0, a_mat)
        q = _dot(a_mat.astype(BF16), block_diag(a_mat))
        levels = L.bit_length() - 2
        for j in range(levels):
            bdq = block_diag(q)
            if j + 1 < levels:
                res = _dot(jnp.concatenate([q.astype(BF16), tm.astype(BF16)], axis=0), bdq)
                q = res[:L]
                tm = tm + res[L:]
            else:
                tm = tm + _dot(tm.astype(BF16), bdq)

        s0 = s_ref[g]
        s0b = s0.astype(BF16)
        v_g = vv[:, sl]
        vm = stack_heads(v_g)
        x1 = _dot_nt(at_g, s0b) + _dot(pa[:, GL:].astype(BF16), vm)
        u = _dot(tm.astype(BF16), stack_heads(x1))
        uvm = jnp.concatenate([stack_heads(u), vm], axis=0)
        y_ref[0, :, sl] = _dot_nt(rt_g, s0b) + _dot(pr.astype(BF16), uvm)
        uv = jnp.concatenate([u.astype(BF16), v_g.astype(BF16)], axis=0)
        bk = jnp.concatenate([bp[:, sl], kp[:, sl]], axis=0)
        ds = _dot_tn(uv, bk)
        s_ref[g] = s0 * wl[:, sl] + jnp.where(same_head, ds, 0.0)


def _wkv(r, lw, k, v, a, b):
    bsz, t, c = r.shape
    L = min(WKV_CHUNK, t)
    w = WKV_HEADS_PER_GROUP * HEAD_SIZE
    assert L & (L - 1) == 0 and t % L == 0 and c % w == 0
    spec = pl.BlockSpec((1, L, c), lambda bi, ci: (bi, ci, 0))
    return pl.pallas_call(
        _wkv_kernel,
        grid=(bsz, t // L),
        in_specs=[spec] * 6,
        out_specs=spec,
        out_shape=jax.ShapeDtypeStruct((bsz, t, c), F32),
        scratch_shapes=[pltpu.VMEM((c // w, w, w), F32)],
        compiler_params=_params("parallel", "arbitrary"),
        name="wkv_scan",
    )(r, lw, k, v, a, b)


def _post_kernel(y_ref, r_ref, k_ref, v_ref, g_ref, vec_ref, e_ref, o_ref):
    e = e_ref[...]
    gn_g, gn_b, r_k = vec_ref[0:1, :], vec_ref[1:2, :], vec_ref[2:3, :]
    inv_n = 1.0 / HEAD_SIZE
    y = y_ref[0]
    d = y - _segsum(y, e) * inv_n
    var = _segsum(d * d, e) * inv_n
    yn = d * lax.rsqrt(var + GN_EPS) * gn_g + gn_b
    bonus = _segsum(r_ref[0] * k_ref[0] * r_k, e)
    o_ref[0] = ((yn + bonus * v_ref[0]) * g_ref[0]).astype(o_ref.dtype)


def _post(y, r, k, v, g, vecs, e, layer, tt):
    b, t, c = y.shape
    tt = min(tt, t)
    spec = pl.BlockSpec((1, tt, c), lambda bi, i: (bi, i, 0))
    return pl.pallas_call(
        _post_kernel,
        grid=(b, t // tt),
        in_specs=[spec] * 5 + [
            pl.BlockSpec((None,) + vecs.shape[1:], lambda bi, i: (layer, 0, 0)),
            pl.BlockSpec(e.shape, lambda bi, i: (0, 0)),
        ],
        out_specs=spec,
        out_shape=jax.ShapeDtypeStruct((b, t, c), BF16),
        compiler_params=_params("parallel", "parallel"),
        name="rwkv_post",
    )(y, r, k, v, g, vecs, e)


def kernel(x, p, attn_norm, w_in, mu_shift, w_vres_dn, mu_vres, v0, v_up, pool_w, pool_scale,
           w0, w_up, a0, a_up, g_up, k_k, k_a, r_k, gn_g, gn_b, w_out, mlp_norm, w_ffn_up,
           w_ffn_down, ple_norm, w_ple_gate, w_ple_proj, final_norm):
    bsz, t, d = x.shape
    depth = w_in.shape[0]
    width = w0.shape[1]
    pool_width = pool_scale.shape[1]
    d_w, d_a, d_g, d_v = w_up.shape[1], a_up.shape[1], g_up.shape[1], v_up.shape[1]
    d_ple = p.shape[-1]
    m = bsz * t
    assert pool_width == width and d_w + d_a == LANE and width % SEG_SLAB == 0
    lora_w = 2 * SEG_SLAB
    w2_rows = lora_w - LANE - LANE
    assert d_g + d_v <= w2_rows
    zw = pool_width + 3 * width + lora_w

    n_in = w_in.shape[2]
    vres = jnp.concatenate([jnp.zeros((1, d, d_v), F32), w_vres_dn], axis=0)
    w_in_x = jnp.concatenate(
        [w_in, vres, jnp.zeros((depth, d, zw - n_in - d_v), F32)], axis=2).astype(BF16)
    mu_vres_x = jnp.concatenate([jnp.zeros((1, d_v), F32), mu_vres], axis=0)
    mu_x = jnp.concatenate(
        [mu_shift, mu_vres_x, jnp.zeros((depth, zw - n_in - d_v), F32)], axis=1)[:, None, :]
    zeros_w = lambda rows: jnp.zeros((depth, rows, width), F32)
    w1 = jnp.concatenate([
        jnp.concatenate([w_up, zeros_w(d_w)], axis=2),
        jnp.concatenate([zeros_w(d_a), a_up], axis=2)], axis=1).astype(BF16)
    v_up_x = jnp.concatenate([jnp.zeros((1, d_v, width), F32), v_up], axis=0)
    w2 = jnp.concatenate([
        jnp.concatenate([g_up, zeros_w(d_g)], axis=2),
        jnp.concatenate([zeros_w(d_v), v_up_x], axis=2),
        jnp.zeros((depth, w2_rows - d_g - d_v, 2 * width), F32)], axis=1).astype(BF16)
    v0_x = jnp.concatenate([jnp.zeros((1, width), F32), v0], axis=0)
    pad3 = jnp.zeros((depth, 3, width), F32)
    prep_vecs = jnp.concatenate(
        [jnp.stack([w0, a0, v0_x, k_k, k_a], axis=1), pad3], axis=1)
    post_vecs = jnp.concatenate(
        [jnp.stack([gn_g, gn_b, r_k.reshape(depth, width)], axis=1), pad3,
         jnp.zeros((depth, 2, width), F32)], axis=1)
    seg = jnp.arange(SEG_SLAB) // HEAD_SIZE
    e = (seg[:, None] == seg[None, :]).astype(BF16)
    attn_g, mlp_g, ple_g = attn_norm[:, None, :], mlp_norm[:, None, :], ple_norm[:, None, :]
    pool_wb = pool_w.astype(BF16)
    pool_sc = pool_scale[:, None, :]
    w_out_b = w_out.astype(BF16).reshape(depth, 2, width, d)
    w_up_b, w_down_b = w_ffn_up.astype(BF16), w_ffn_down.astype(BF16)
    w_gate_b, w_proj_b = w_ple_gate.astype(BF16), w_ple_proj.astype(BF16)
    p2 = p.reshape(depth * m, d_ple)
    final_g = final_norm[None, :]

    xf = x.reshape(m, d)
    v_first = None
    for i in range(depth):
        z = _rms_matmul(xf, attn_g, w_in_x, i, tm=1024, tn=768).reshape(bsz, t, zw)
        pool_out = _pool(z, pool_wb, pool_sc, i, pool_width, tt=512)
        first = i == 0
        r, lw, k, v, a, b, g = _prep(z, mu_x, prep_vecs, w1, w2, e, z if first else v_first,
                                     i, first, width, d_w, d_g, tt=256)
        if first:
            v_first = v
        y = _wkv(r, lw, k, v, a, b)
        rwkv_out = _post(y, r, k, v, g, post_vecs, e, i, tt=512)
        xf = _outproj(xf, pool_out.reshape(m, pool_width), rwkv_out.reshape(m, width), w_out_b, i,
                      tm=1024, tn=1024)
        xf = _ffn(xf, mlp_g, w_up_b, w_down_b, i, tm=512, tf=1024)
        xf = _ple(xf, ple_g, w_gate_b, p2, w_proj_b, final_g, i, i == depth - 1, tm=512)
    return xf.reshape(bsz, t, d)
```

```python
import functools

import jax
import jax.numpy as jnp
from jax import lax
from jax.experimental import pallas as pl
from jax.experimental.pallas import tpu as pltpu

F32 = jnp.float32
BF16 = jnp.bfloat16

NORM_EPS = 1e-6
GN_EPS = 64e-5
HEAD_SIZE = 64
POOL_WINDOWS = (2, 4, 8, 16)
POOL_HALO = 16
SHIFT_HALO = 8
WKV_CHUNK = 64
WKV_HEADS_PER_GROUP = 2
SEG_SLAB = 256
LANE = 128
VMEM_LIMIT = 52 * 1024 * 1024


def _dot(a, b):
    return jnp.dot(a, b, preferred_element_type=F32)


def _dot_nt(a, b):
    return lax.dot_general(a, b, (((1,), (1,)), ((), ())), preferred_element_type=F32)


def _dot_tn(a, b):
    return lax.dot_general(a, b, (((0,), (0,)), ((), ())), preferred_element_type=F32)


def _params(*sem):
    return pltpu.CompilerParams(dimension_semantics=sem, vmem_limit_bytes=VMEM_LIMIT)


def _rms_rows(x, g):
    ms = jnp.mean(x * x, axis=-1, keepdims=True)
    return x * lax.rsqrt(ms + NORM_EPS) * g


def _norm_into(h_ref, x_ref, g_ref, rows=256):
    rows = min(rows, x_ref.shape[0])
    n = x_ref.shape[0] // rows

    def body(c, carry):
        rs = pl.ds(pl.multiple_of(c * rows, rows), rows)
        h_ref[rs, :] = _rms_rows(x_ref[rs, :], g_ref[...]).astype(h_ref.dtype)
        return carry

    lax.fori_loop(0, n, body, 0)


def _segsum(x, e):
    outs = []
    for s in range(x.shape[1] // SEG_SLAB):
        xs = x[:, s * SEG_SLAB:(s + 1) * SEG_SLAB]
        hi = xs.astype(BF16)
        lo = (xs - hi.astype(F32)).astype(BF16)
        outs.append(_dot(hi, e) + _dot(lo, e))
    return jnp.concatenate(outs, axis=1)


def _rms_matmul_kernel(x_ref, g_ref, w_ref, o_ref, h_ref):
    @pl.when(pl.program_id(1) == 0)
    def _():
        _norm_into(h_ref, x_ref, g_ref)

    o_ref[...] = _dot(h_ref[...], w_ref[...])


def _rms_matmul(x, g, w, layer, tm, tn):
    m, d = x.shape
    n = w.shape[2]
    tm, tn = min(tm, m), min(tn, n)
    return pl.pallas_call(
        _rms_matmul_kernel,
        grid=(m // tm, n // tn),
        in_specs=[
            pl.BlockSpec((tm, d), lambda i, j: (i, 0)),
            pl.BlockSpec((None, 1, d), lambda i, j: (layer, 0, 0)),
            pl.BlockSpec((None, d, tn), lambda i, j: (layer, 0, j)),
        ],
        out_specs=pl.BlockSpec((tm, tn), lambda i, j: (i, j)),
        out_shape=jax.ShapeDtypeStruct((m, n), F32),
        scratch_shapes=[pltpu.VMEM((tm, d), BF16)],
        compiler_params=_params("parallel", "arbitrary"),
        name="rms_matmul",
    )(x, g, w)


def _outproj_kernel(x_ref, ma_ref, mb_ref, wa_ref, wb_ref, o_ref):
    o_ref[...] = x_ref[...] + _dot(ma_ref[...], wa_ref[...]) + _dot(mb_ref[...], wb_ref[...])


def _outproj(x, mix_a, mix_b, w, layer, tm, tn):
    m, d = x.shape
    ka, kb = mix_a.shape[1], mix_b.shape[1]
    tm, tn = min(tm, m), min(tn, d)
    assert ka == kb
    return pl.pallas_call(
        _outproj_kernel,
        grid=(m // tm, d // tn),
        in_specs=[
            pl.BlockSpec((tm, tn), lambda i, j: (i, j)),
            pl.BlockSpec((tm, ka), lambda i, j: (i, 0)),
            pl.BlockSpec((tm, kb), lambda i, j: (i, 0)),
            pl.BlockSpec((None, None, ka, tn), lambda i, j: (layer, 0, 0, j)),
            pl.BlockSpec((None, None, kb, tn), lambda i, j: (layer, 1, 0, j)),
        ],
        out_specs=pl.BlockSpec((tm, tn), lambda i, j: (i, j)),
        out_shape=jax.ShapeDtypeStruct((m, d), F32),
        compiler_params=_params("parallel", "parallel"),
        name="outproj",
    )(x, mix_a, mix_b, w, w)


def _ffn_kernel(x_ref, g_ref, wu_ref, wd_ref, o_ref, h_ref):
    @pl.when(pl.program_id(1) == 0)
    def _():
        _norm_into(h_ref, x_ref, g_ref)
        o_ref[...] = x_ref[...]

    a = _dot(h_ref[...], wu_ref[...])
    a = jnp.square(jnp.maximum(a, 0.0)).astype(BF16)
    o_ref[...] += _dot(a, wd_ref[...])


def _ffn(x, g, w_up, w_down, layer, tm, tf):
    m, d = x.shape
    f = w_up.shape[2]
    tm, tf = min(tm, m), min(tf, f)
    return pl.pallas_call(
        _ffn_kernel,
        grid=(m // tm, f // tf),
        in_specs=[
            pl.BlockSpec((tm, d), lambda i, j: (i, 0)),
            pl.BlockSpec((None, 1, d), lambda i, j: (layer, 0, 0)),
            pl.BlockSpec((None, d, tf), lambda i, j: (layer, 0, j)),
            pl.BlockSpec((None, tf, d), lambda i, j: (layer, j, 0)),
        ],
        out_specs=pl.BlockSpec((tm, d), lambda i, j: (i, 0)),
        out_shape=jax.ShapeDtypeStruct((m, d), F32),
        scratch_shapes=[pltpu.VMEM((tm, d), BF16)],
        compiler_params=_params("parallel", "arbitrary"),
        name="ffn",
    )(x, g, w_up, w_down)


def _ple_kernel(final, tn, x_ref, g_ref, wg_ref, p_ref, wp_ref, fg_ref, o_ref, h_ref):
    _norm_into(h_ref, x_ref, g_ref)
    pb = p_ref[...].astype(BF16)
    for c in range(x_ref.shape[1] // tn):
        cs = slice(c * tn, (c + 1) * tn)
        gate = jax.nn.sigmoid(_dot(h_ref[...], wg_ref[:, cs]))
        o_ref[:, cs] = x_ref[:, cs] + gate * _dot(pb, wp_ref[:, cs])
    if final:
        _norm_into(o_ref, o_ref, fg_ref)


def _ple(x, g, w_gate, p, w_proj, final_g, layer, final, tm):
    m, d = x.shape
    dp = p.shape[1]
    tm = min(tm, m)
    nblk = m // tm
    return pl.pallas_call(
        functools.partial(_ple_kernel, final, min(512, d)),
        grid=(nblk,),
        in_specs=[
            pl.BlockSpec((tm, d), lambda i: (i, 0)),
            pl.BlockSpec((None, 1, d), lambda i: (layer, 0, 0)),
            pl.BlockSpec((None, d, d), lambda i: (layer, 0, 0)),
            pl.BlockSpec((tm, dp), lambda i: (layer * nblk + i, 0)),
            pl.BlockSpec((None, dp, d), lambda i: (layer, 0, 0)),
            pl.BlockSpec((1, d), lambda i: (0, 0)),
        ],
        out_specs=pl.BlockSpec((tm, d), lambda i: (i, 0)),
        out_shape=jax.ShapeDtypeStruct((m, d), F32),
        scratch_shapes=[pltpu.VMEM((tm, d), BF16)],
        compiler_params=_params("parallel"),
        name="ple",
    )(x, g, w_gate, p, w_proj, final_g)


def _pool_kernel(z_ref, halo_ref, pw_ref, ps_ref, o_ref):
    i = pl.program_id(1)
    tt, width = z_ref.shape[1], z_ref.shape[2]
    pg = width // len(POOL_WINDOWS)
    u = z_ref[0]
    halo = jnp.where(i == 0, 0.0, halo_ref[0])
    cur = jnp.concatenate([halo, u], axis=0)
    span, lane0 = 1, 0
    t = i * tt + lax.broadcasted_iota(jnp.int32, (tt, pg), 0)
    for gi, win in enumerate(POOL_WINDOWS):
        cur = cur[:, gi * pg - lane0:]
        lane0 = gi * pg
        while span < win:
            cur = cur + pltpu.roll(cur, span, 0)
            span *= 2
        cols = slice(gi * pg, (gi + 1) * pg)
        cnt = jnp.minimum(t + 1, win).astype(F32)
        d = cur[POOL_HALO:, :pg] / cnt - u[:, cols]
        o = _dot(d.astype(BF16), pw_ref[gi]) * ps_ref[:, cols]
        o_ref[0, :, cols] = o.astype(o_ref.dtype)


def _pool(z, pool_w, pool_scale, layer, width, tt):
    b, t, _ = z.shape
    g, pg, _ = pool_w.shape[1:]
    tt = min(tt, t)
    assert all(w & (w - 1) == 0 for w in POOL_WINDOWS) and max(POOL_WINDOWS) <= POOL_HALO
    assert list(POOL_WINDOWS) == sorted(POOL_WINDOWS) and g == len(POOL_WINDOWS) and g * pg == width
    hb = tt // POOL_HALO
    return pl.pallas_call(
        _pool_kernel,
        grid=(b, t // tt),
        in_specs=[
            pl.BlockSpec((1, tt, width), lambda bi, i: (bi, i, 0)),
            pl.BlockSpec((1, POOL_HALO, width), lambda bi, i: (bi, jnp.maximum(i * hb - 1, 0), 0)),
            pl.BlockSpec((None, g, pg, pg), lambda bi, i: (layer, 0, 0, 0)),
            pl.BlockSpec((None, 1, width), lambda bi, i: (layer, 0, 0)),
        ],
        out_specs=pl.BlockSpec((1, tt, width), lambda bi, i: (bi, i, 0)),
        out_shape=jax.ShapeDtypeStruct((b, t, width), BF16),
        compiler_params=_params("parallel", "parallel"),
        name="pool",
    )(z, z, pool_w, pool_scale)


def _prep_kernel(first, dims, zr_ref, zk_ref, zv_ref, zl_ref, hr_ref, hk_ref, hv_ref, hl_ref,
                 mu_ref, vec_ref, w1_ref, w2_ref, e_ref, vf_ref,
                 r_o, lw_o, k_o, v_o, a_o, b_o, g_o):
    width, d_w, d_g = dims
    i = pl.program_id(1)

    def shifted(z_ref, h_ref, mu):
        z = z_ref[0]
        prev = jnp.where(i == 0, 0.0, h_ref[0, SHIFT_HALO - 1:SHIFT_HALO, :])
        zp = pltpu.roll(z, 1, 0)
        row = lax.broadcasted_iota(jnp.int32, z.shape, 0)
        zp = jnp.where(row == 0, prev, zp)
        return z + (zp - z) * mu

    w0, a0, v0 = vec_ref[0:1, :], vec_ref[1:2, :], vec_ref[2:3, :]
    k_k, k_a = vec_ref[3:4, :], vec_ref[4:5, :]

    lz = shifted(zl_ref, hl_ref, mu_ref[:, 3 * width:])
    s1 = lz[:, :LANE]
    lane1 = lax.broadcasted_iota(jnp.int32, s1.shape, 1)
    act1 = jnp.where(lane1 < d_w, jnp.tanh(s1), s1)
    lo1 = _dot(act1.astype(BF16), w1_ref[...])
    s2 = lz[:, LANE:LANE + w2_ref.shape[0]]
    lane2 = lax.broadcasted_iota(jnp.int32, s2.shape, 1)
    act2 = jnp.where(lane2 < d_g, jax.nn.sigmoid(s2), s2)
    lo2 = _dot(act2.astype(BF16), w2_ref[...])

    u = w0 + lo1[:, :width]
    w = -(jnp.maximum(-u, 0.0) + jnp.log1p(jnp.exp(-jnp.abs(u)))) - 0.5
    lw_o[0] = -jnp.exp(w)
    a = jax.nn.sigmoid(a0 + lo1[:, width:])
    g_o[0] = lo2[:, :width]

    r_o[0] = shifted(zr_ref, hr_ref, mu_ref[:, :width])
    k = shifted(zk_ref, hk_ref, mu_ref[:, width:2 * width])
    v = shifted(zv_ref, hv_ref, mu_ref[:, 2 * width:3 * width])
    if not first:
        v = v + (vf_ref[0] - v) * jax.nn.sigmoid(v0 + lo2[:, width:])
    v_o[0] = v

    x = k * k_k
    kk = x / jnp.maximum(jnp.sqrt(_segsum(x * x, e_ref[...])), 1e-12)
    k_o[0] = k * (1.0 + (a - 1.0) * k_a)
    a_o[0] = -kk
    b_o[0] = kk * a


def _prep(z, mu, vecs, w1, w2, e, v_first, layer, first, width, d_w, d_g, tt):
    b, t, zw = z.shape
    tt = min(tt, t)
    lw = zw - 4 * width
    hb = tt // SHIFT_HALO
    lcol = 4 * width // lw
    assert 4 * width % lw == 0

    def main(col, w):
        return pl.BlockSpec((1, tt, w), lambda bi, i: (bi, i, col))

    def halo(col, w):
        return pl.BlockSpec((1, SHIFT_HALO, w), lambda bi, i: (bi, jnp.maximum(i * hb - 1, 0), col))

    def whole(a):
        return pl.BlockSpec((None,) + a.shape[1:], lambda bi, i: (layer,) + (0,) * (a.ndim - 1))

    out = jax.ShapeDtypeStruct((b, t, width), F32)
    return pl.pallas_call(
        functools.partial(_prep_kernel, first, (width, d_w, d_g)),
        grid=(b, t // tt),
        in_specs=[
            main(1, width), main(2, width), main(3, width), main(lcol, lw),
            halo(1, width), halo(2, width), halo(3, width), halo(lcol, lw),
            whole(mu), whole(vecs), whole(w1), whole(w2),
            pl.BlockSpec(e.shape, lambda bi, i: (0, 0)),
            pl.BlockSpec((1, tt, width), lambda bi, i: (bi, i, 0)),
        ],
        out_specs=[pl.BlockSpec((1, tt, width), lambda bi, i: (bi, i, 0))] * 7,
        out_shape=[out] * 7,
        compiler_params=_params("parallel", "parallel"),
        name="rwkv_prep",
    )(z, z, z, z, z, z, z, z, mu, vecs, w1, w2, e, v_first)


def _wkv_kernel(r_ref, lw_ref, k_ref, v_ref, a_ref, b_ref, y_ref, s_ref):
    hg = WKV_HEADS_PER_GROUP
    c = pl.program_id(1)
    L, C = r_ref.shape[1], r_ref.shape[2]
    W, GL = hg * HEAD_SIZE, hg * L
    ng = C // W

    @pl.when(c == 0)
    def _():
        s_ref[...] = jnp.zeros_like(s_ref)

    lw = lw_ref[0]
    ti = lax.broadcasted_iota(jnp.int32, (L, L), 0)
    tj = lax.broadcasted_iota(jnp.int32, (L, L), 1)
    tri = jnp.where(tj <= ti, 1.0, 0.0).astype(BF16)
    h1 = lw.astype(BF16)
    r1 = lw - h1.astype(F32)
    h2 = r1.astype(BF16)
    h3 = (r1 - h2.astype(F32)).astype(BF16)
    cum = _dot(tri, h1) + _dot(tri, h2) + _dot(tri, h3)
    tot = cum[L - 1:L, :]
    e_inv = jnp.exp(-cum)
    e_end = jnp.exp(tot - cum)
    wl = jnp.exp(tot)
    at = a_ref[0] * jnp.exp(cum - lw)
    rt = r_ref[0] * jnp.exp(cum)
    bt = b_ref[0] * e_inv
    kt = k_ref[0] * e_inv
    bp = (b_ref[0] * e_end).astype(BF16)
    kp = (k_ref[0] * e_end).astype(BF16)
    vv = v_ref[0]

    lane = lax.broadcasted_iota(jnp.int32, (L, W), 1)
    head_mask = [(lane >= h * HEAD_SIZE) & (lane < (h + 1) * HEAD_SIZE) for h in range(hg)]
    row2 = lax.broadcasted_iota(jnp.int32, (L, 2 * GL), 0)
    col2 = lax.broadcasted_iota(jnp.int32, (L, 2 * GL), 1) & (L - 1)
    strict = col2 < row2
    incl = col2 <= row2
    colg = lax.broadcasted_iota(jnp.int32, (L, GL), 1)
    rowg = lax.broadcasted_iota(jnp.int32, (L, GL), 0)
    eye = (colg & (L - 1)) == rowg
    blk_mask = [(colg >= h * L) & (colg < (h + 1) * L) for h in range(hg)]
    sr = lax.broadcasted_iota(jnp.int32, (W, W), 0)
    sc = lax.broadcasted_iota(jnp.int32, (W, W), 1)
    same_head = (sr // HEAD_SIZE) == (sc // HEAD_SIZE)

    def stack_heads(x):
        return jnp.concatenate([jnp.where(m, x, 0.0).astype(BF16) for m in head_mask], axis=0)

    def block_diag(q):
        return jnp.concatenate([jnp.where(m, q, 0.0).astype(BF16) for m in blk_mask], axis=0)

    groups = range(ng)
    sls = [slice(g * W, (g + 1) * W) for g in groups]
    lhs = [jnp.concatenate([at[:, sl].astype(BF16), rt[:, sl].astype(BF16)], axis=0) for sl in sls]
    gm = [_dot_nt(lhs[g], jnp.concatenate([stack_heads(bt[:, sls[g]]), stack_heads(kt[:, sls[g]])],
                                          axis=0)) for g in groups]
    s0 = [s_ref[g] for g in groups]
    as0 = [_dot_nt(lhs[g], s0[g].astype(BF16)) for g in groups]
    pa = [jnp.where(strict, gm[g][:L], 0.0) for g in groups]
    pr = [jnp.where(incl, gm[g][L:], 0.0) for g in groups]
    vm = [stack_heads(vv[:, sl]) for sl in sls]

    tm = [jnp.where(eye, 1.0, pa[g][:, :GL]) for g in groups]
    q = [_dot(pa[g][:, :GL].astype(BF16), block_diag(pa[g][:, :GL])) for g in groups]
    x1 = [as0[g][:L] + _dot(pa[g][:, GL:].astype(BF16), vm[g]) for g in groups]
    levels = L.bit_length() - 2
    for j in range(levels):
        if j + 1 < levels:
            res = [_dot(jnp.concatenate([q[g].astype(BF16), tm[g].astype(BF16)], axis=0),
                        block_diag(q[g])) for g in groups]
            q = [res[g][:L] for g in groups]
            tm = [tm[g] + res[g][L:] for g in groups]
        else:
            tm = [tm[g] + _dot(tm[g].astype(BF16), block_diag(q[g])) for g in groups]

    u = [_dot(tm[g].astype(BF16), stack_heads(x1[g])) for g in groups]
    for g in groups:
        uvm = jnp.concatenate([stack_heads(u[g]), vm[g]], axis=0)
        y_ref[0, :, sls[g]] = as0[g][L:] + _dot(pr[g].astype(BF16), uvm)
    for g in groups:
        uv = jnp.concatenate([u[g].astype(BF16), vv[:, sls[g]].astype(BF16)], axis=0)
        bk = jnp.concatenate([bp[:, sls[g]], kp[:, sls[g]]], axis=0)
        ds = _dot_tn(uv, bk)
        s_ref[g] = s0[g] * wl[:, sls[g]] + jnp.where(same_head, ds, 0.0)


def _wkv(r, lw, k, v, a, b):
    bsz, t, c = r.shape
    L = min(WKV_CHUNK, t)
    w = WKV_HEADS_PER_GROUP * HEAD_SIZE
    assert L & (L - 1) == 0 and t % L == 0 and c % w == 0
    spec = pl.BlockSpec((1, L, c), lambda bi, ci: (bi, ci, 0))
    return pl.pallas_call(
        _wkv_kernel,
        grid=(bsz, t // L),
        in_specs=[spec] * 6,
        out_specs=spec,
        out_shape=jax.ShapeDtypeStruct((bsz, t, c), F32),
        scratch_shapes=[pltpu.VMEM((c // w, w, w), F32)],
        compiler_params=_params("parallel", "arbitrary"),
        name="wkv_scan",
    )(r, lw, k, v, a, b)


def _post_kernel(y_ref, r_ref, k_ref, v_ref, g_ref, vec_ref, e_ref, o_ref):
    e = e_ref[...]
    gn_g, gn_b, r_k = vec_ref[0:1, :], vec_ref[1:2, :], vec_ref[2:3, :]
    inv_n = 1.0 / HEAD_SIZE
    y = y_ref[0]
    d = y - _segsum(y, e) * inv_n
    var = _segsum(d * d, e) * inv_n
    yn = d * lax.rsqrt(var + GN_EPS) * gn_g + gn_b
    bonus = _segsum(r_ref[0] * k_ref[0] * r_k, e)
    o_ref[0] = ((yn + bonus * v_ref[0]) * g_ref[0]).astype(o_ref.dtype)


def _post(y, r, k, v, g, vecs, e, layer, tt):
    b, t, c = y.shape
    tt = min(tt, t)
    spec = pl.BlockSpec((1, tt, c), lambda bi, i: (bi, i, 0))
    return pl.pallas_call(
        _post_kernel,
        grid=(b, t // tt),
        in_specs=[spec] * 5 + [
            pl.BlockSpec((None,) + vecs.shape[1:], lambda bi, i: (layer, 0, 0)),
            pl.BlockSpec(e.shape, lambda bi, i: (0, 0)),
        ],
        out_specs=spec,
        out_shape=jax.ShapeDtypeStruct((b, t, c), BF16),
        compiler_params=_params("parallel", "parallel"),
        name="rwkv_post",
    )(y, r, k, v, g, vecs, e)


def kernel(x, p, attn_norm, w_in, mu_shift, w_vres_dn, mu_vres, v0, v_up, pool_w, pool_scale,
           w0, w_up, a0, a_up, g_up, k_k, k_a, r_k, gn_g, gn_b, w_out, mlp_norm, w_ffn_up,
           w_ffn_down, ple_norm, w_ple_gate, w_ple_proj, final_norm):
    bsz, t, d = x.shape
    depth = w_in.shape[0]
    width = w0.shape[1]
    pool_width = pool_scale.shape[1]
    d_w, d_a, d_g, d_v = w_up.shape[1], a_up.shape[1], g_up.shape[1], v_up.shape[1]
    d_ple = p.shape[-1]
    m = bsz * t
    assert pool_width == width and d_w + d_a == LANE and width % SEG_SLAB == 0
    lora_w = 2 * SEG_SLAB
    w2_rows = lora_w - LANE - LANE
    assert d_g + d_v <= w2_rows
    zw = pool_width + 3 * width + lora_w

    n_in = w_in.shape[2]
    vres = jnp.concatenate([jnp.zeros((1, d, d_v), F32), w_vres_dn], axis=0)
    w_in_x = jnp.concatenate(
        [w_in, vres, jnp.zeros((depth, d, zw - n_in - d_v), F32)], axis=2).astype(BF16)
    mu_vres_x = jnp.concatenate([jnp.zeros((1, d_v), F32), mu_vres], axis=0)
    mu_x = jnp.concatenate(
        [mu_shift, mu_vres_x, jnp.zeros((depth, zw - n_in - d_v), F32)], axis=1)[:, None, :]
    zeros_w = lambda rows: jnp.zeros((depth, rows, width), F32)
    w1 = jnp.concatenate([
        jnp.concatenate([w_up, zeros_w(d_w)], axis=2),
        jnp.concatenate([zeros_w(d_a), a_up], axis=2)], axis=1).astype(BF16)
    v_up_x = jnp.concatenate([jnp.zeros((1, d_v, width), F32), v_up], axis=0)
    w2 = jnp.concatenate([
        jnp.concatenate([g_up, zeros_w(d_g)], axis=2),
        jnp.concatenate([zeros_w(d_v), v_up_x], axis=2),
        jnp.zeros((depth, w2_rows - d_g - d_v, 2 * width), F32)], axis=1).astype(BF16)
    v0_x = jnp.concatenate([jnp.zeros((1, width), F32), v0], axis=0)
    pad3 = jnp.zeros((depth, 3, width), F32)
    prep_vecs = jnp.concatenate(
        [jnp.stack([w0, a0, v0_x, k_k, k_a], axis=1), pad3], axis=1)
    post_vecs = jnp.concatenate(
        [jnp.stack([gn_g, gn_b, r_k.reshape(depth, width)], axis=1), pad3,
         jnp.zeros((depth, 2, width), F32)], axis=1)
    seg = jnp.arange(SEG_SLAB) // HEAD_SIZE
    e = (seg[:, None] == seg[None, :]).astype(BF16)
    attn_g, mlp_g, ple_g = attn_norm[:, None, :], mlp_norm[:, None, :], ple_norm[:, None, :]
    pool_wb = pool_w.astype(BF16)
    pool_sc = pool_scale[:, None, :]
    w_out_b = w_out.astype(BF16).reshape(depth, 2, width, d)
    w_up_b, w_down_b = w_ffn_up.astype(BF16), w_ffn_down.astype(BF16)
    w_gate_b, w_proj_b = w_ple_gate.astype(BF16), w_ple_proj.astype(BF16)
    p2 = p.reshape(depth * m, d_ple)
    final_g = final_norm[None, :]

    xf = x.reshape(m, d)
    v_first = None
    for i in range(depth):
        z = _rms_matmul(xf, attn_g, w_in_x, i, tm=1024, tn=768).reshape(bsz, t, zw)
        pool_out = _pool(z, pool_wb, pool_sc, i, pool_width, tt=512)
        first = i == 0
        r, lw, k, v, a, b, g = _prep(z, mu_x, prep_vecs, w1, w2, e, z if first else v_first,
                                     i, first, width, d_w, d_g, tt=256)
        if first:
            v_first = v
        y = _wkv(r, lw, k, v, a, b)
        rwkv_out = _post(y, r, k, v, g, post_vecs, e, i, tt=512)
        xf = _outproj(xf, pool_out.reshape(m, pool_width), rwkv_out.reshape(m, width), w_out_b, i,
                      tm=1024, tn=1024)
        xf = _ffn(xf, mlp_g, w_up_b, w_down_b, i, tm=512, tf=1024)
        xf = _ple(xf, ple_g, w_gate_b, p2, w_proj_b, final_g, i, i == depth - 1, tm=512)
    return xf.reshape(bsz, t, d)
```

```python
import functools

import jax
import jax.numpy as jnp
from jax import lax
from jax.experimental import pallas as pl
from jax.experimental.pallas import tpu as pltpu

F32 = jnp.float32
BF16 = jnp.bfloat16

NORM_EPS = 1e-6
GN_EPS = 64e-5
HEAD_SIZE = 64
POOL_WINDOWS = (2, 4, 8, 16)
POOL_HALO = 16
SHIFT_HALO = 8
WKV_CHUNK = 64
WKV_HEADS_PER_GROUP = 2
WKV_ROWS_PER_STEP = 2
SEG_SLAB = 256
LANE = 128
VMEM_LIMIT = 52 * 1024 * 1024


def _dot(a, b):
    return jnp.dot(a, b, preferred_element_type=F32)


def _dot_nt(a, b):
    return lax.dot_general(a, b, (((1,), (1,)), ((), ())), preferred_element_type=F32)


def _dot_tn(a, b):
    return lax.dot_general(a, b, (((0,), (0,)), ((), ())), preferred_element_type=F32)


def _params(*sem):
    return pltpu.CompilerParams(dimension_semantics=sem, vmem_limit_bytes=VMEM_LIMIT)


def _rms_rows(x, g):
    ms = jnp.mean(x * x, axis=-1, keepdims=True)
    return x * lax.rsqrt(ms + NORM_EPS) * g


def _norm_into(h_ref, x_ref, g_ref, rows=256):
    rows = min(rows, x_ref.shape[0])
    n = x_ref.shape[0] // rows

    def body(c, carry):
        rs = pl.ds(pl.multiple_of(c * rows, rows), rows)
        h_ref[rs, :] = _rms_rows(x_ref[rs, :], g_ref[...]).astype(h_ref.dtype)
        return carry

    lax.fori_loop(0, n, body, 0)


def _segsum(x, e):
    outs = []
    for s in range(x.shape[1] // SEG_SLAB):
        xs = x[:, s * SEG_SLAB:(s + 1) * SEG_SLAB]
        hi = xs.astype(BF16)
        lo = (xs - hi.astype(F32)).astype(BF16)
        outs.append(_dot(hi, e) + _dot(lo, e))
    return jnp.concatenate(outs, axis=1)


def _rms_matmul_kernel(x_ref, g_ref, w_ref, o_ref, h_ref):
    @pl.when(pl.program_id(1) == 0)
    def _():
        _norm_into(h_ref, x_ref, g_ref)

    o_ref[...] = _dot(h_ref[...], w_ref[...])


def _rms_matmul(x, g, w, layer, tm, tn):
    m, d = x.shape
    n = w.shape[2]
    tm, tn = min(tm, m), min(tn, n)
    return pl.pallas_call(
        _rms_matmul_kernel,
        grid=(m // tm, n // tn),
        in_specs=[
            pl.BlockSpec((tm, d), lambda i, j: (i, 0)),
            pl.BlockSpec((None, 1, d), lambda i, j: (layer, 0, 0)),
            pl.BlockSpec((None, d, tn), lambda i, j: (layer, 0, j)),
        ],
        out_specs=pl.BlockSpec((tm, tn), lambda i, j: (i, j)),
        out_shape=jax.ShapeDtypeStruct((m, n), F32),
        scratch_shapes=[pltpu.VMEM((tm, d), BF16)],
        compiler_params=_params("parallel", "arbitrary"),
        name="rms_matmul",
    )(x, g, w)


def _outproj_kernel(x_ref, ma_ref, mb_ref, wa_ref, wb_ref, o_ref):
    o_ref[...] = x_ref[...] + _dot(ma_ref[...], wa_ref[...]) + _dot(mb_ref[...], wb_ref[...])


def _outproj(x, mix_a, mix_b, w, layer, tm, tn):
    m, d = x.shape
    ka, kb = mix_a.shape[1], mix_b.shape[1]
    tm, tn = min(tm, m), min(tn, d)
    assert ka == kb
    return pl.pallas_call(
        _outproj_kernel,
        grid=(m // tm, d // tn),
        in_specs=[
            pl.BlockSpec((tm, tn), lambda i, j: (i, j)),
            pl.BlockSpec((tm, ka), lambda i, j: (i, 0)),
            pl.BlockSpec((tm, kb), lambda i, j: (i, 0)),
            pl.BlockSpec((None, None, ka, tn), lambda i, j: (layer, 0, 0, j)),
            pl.BlockSpec((None, None, kb, tn), lambda i, j: (layer, 1, 0, j)),
        ],
        out_specs=pl.BlockSpec((tm, tn), lambda i, j: (i, j)),
        out_shape=jax.ShapeDtypeStruct((m, d), F32),
        compiler_params=_params("parallel", "parallel"),
        name="outproj",
    )(x, mix_a, mix_b, w, w)


def _ffn_kernel(x_ref, g_ref, wu_ref, wd_ref, o_ref, h_ref):
    @pl.when(pl.program_id(1) == 0)
    def _():
        _norm_into(h_ref, x_ref, g_ref)
        o_ref[...] = x_ref[...]

    a = _dot(h_ref[...], wu_ref[...])
    a = jnp.square(jnp.maximum(a, 0.0)).astype(BF16)
    o_ref[...] += _dot(a, wd_ref[...])


def _ffn(x, g, w_up, w_down, layer, tm, tf):
    m, d = x.shape
    f = w_up.shape[2]
    tm, tf = min(tm, m), min(tf, f)
    return pl.pallas_call(
        _ffn_kernel,
        grid=(m // tm, f // tf),
        in_specs=[
            pl.BlockSpec((tm, d), lambda i, j: (i, 0)),
            pl.BlockSpec((None, 1, d), lambda i, j: (layer, 0, 0)),
            pl.BlockSpec((None, d, tf), lambda i, j: (layer, 0, j)),
            pl.BlockSpec((None, tf, d), lambda i, j: (layer, j, 0)),
        ],
        out_specs=pl.BlockSpec((tm, d), lambda i, j: (i, 0)),
        out_shape=jax.ShapeDtypeStruct((m, d), F32),
        scratch_shapes=[pltpu.VMEM((tm, d), BF16)],
        compiler_params=_params("parallel", "arbitrary"),
        name="ffn",
    )(x, g, w_up, w_down)


def _ple_kernel(final, tn, x_ref, g_ref, wg_ref, p_ref, wp_ref, fg_ref, o_ref, h_ref):
    _norm_into(h_ref, x_ref, g_ref)
    pb = p_ref[...].astype(BF16)
    for c in range(x_ref.shape[1] // tn):
        cs = slice(c * tn, (c + 1) * tn)
        gate = jax.nn.sigmoid(_dot(h_ref[...], wg_ref[:, cs]))
        o_ref[:, cs] = x_ref[:, cs] + gate * _dot(pb, wp_ref[:, cs])
    if final:
        _norm_into(o_ref, o_ref, fg_ref)


def _ple(x, g, w_gate, p, w_proj, final_g, layer, final, tm):
    m, d = x.shape
    dp = p.shape[1]
    tm = min(tm, m)
    nblk = m // tm
    return pl.pallas_call(
        functools.partial(_ple_kernel, final, min(512, d)),
        grid=(nblk,),
        in_specs=[
            pl.BlockSpec((tm, d), lambda i: (i, 0)),
            pl.BlockSpec((None, 1, d), lambda i: (layer, 0, 0)),
            pl.BlockSpec((None, d, d), lambda i: (layer, 0, 0)),
            pl.BlockSpec((tm, dp), lambda i: (layer * nblk + i, 0)),
            pl.BlockSpec((None, dp, d), lambda i: (layer, 0, 0)),
            pl.BlockSpec((1, d), lambda i: (0, 0)),
        ],
        out_specs=pl.BlockSpec((tm, d), lambda i: (i, 0)),
        out_shape=jax.ShapeDtypeStruct((m, d), F32),
        scratch_shapes=[pltpu.VMEM((tm, d), BF16)],
        compiler_params=_params("parallel"),
        name="ple",
    )(x, g, w_gate, p, w_proj, final_g)


def _pool_kernel(z_ref, halo_ref, pw_ref, ps_ref, o_ref):
    i = pl.program_id(1)
    tt, width = z_ref.shape[1], z_ref.shape[2]
    pg = width // len(POOL_WINDOWS)
    u = z_ref[0]
    halo = jnp.where(i == 0, 0.0, halo_ref[0])
    cur = jnp.concatenate([halo, u], axis=0)
    span, lane0 = 1, 0
    t = i * tt + lax.broadcasted_iota(jnp.int32, (tt, pg), 0)
    for gi, win in enumerate(POOL_WINDOWS):
        cur = cur[:, gi * pg - lane0:]
        lane0 = gi * pg
        while span < win:
            cur = cur + pltpu.roll(cur, span, 0)
            span *= 2
        cols = slice(gi * pg, (gi + 1) * pg)
        cnt = jnp.minimum(t + 1, win).astype(F32)
        d = cur[POOL_HALO:, :pg] / cnt - u[:, cols]
        o = _dot(d.astype(BF16), pw_ref[gi]) * ps_ref[:, cols]
        o_ref[0, :, cols] = o.astype(o_ref.dtype)


def _pool(z, pool_w, pool_scale, layer, width, tt):
    b, t, _ = z.shape
    g, pg, _ = pool_w.shape[1:]
    tt = min(tt, t)
    assert all(w & (w - 1) == 0 for w in POOL_WINDOWS) and max(POOL_WINDOWS) <= POOL_HALO
    assert list(POOL_WINDOWS) == sorted(POOL_WINDOWS) and g == len(POOL_WINDOWS) and g * pg == width
    hb = tt // POOL_HALO
    return pl.pallas_call(
        _pool_kernel,
        grid=(b, t // tt),
        in_specs=[
            pl.BlockSpec((1, tt, width), lambda bi, i: (bi, i, 0)),
            pl.BlockSpec((1, POOL_HALO, width), lambda bi, i: (bi, jnp.maximum(i * hb - 1, 0), 0)),
            pl.BlockSpec((None, g, pg, pg), lambda bi, i: (layer, 0, 0, 0)),
            pl.BlockSpec((None, 1, width), lambda bi, i: (layer, 0, 0)),
        ],
        out_specs=pl.BlockSpec((1, tt, width), lambda bi, i: (bi, i, 0)),
        out_shape=jax.ShapeDtypeStruct((b, t, width), BF16),
        compiler_params=_params("parallel", "parallel"),
        name="pool",
    )(z, z, pool_w, pool_scale)


def _prep_kernel(first, dims, zr_ref, zk_ref, zv_ref, zl_ref, hr_ref, hk_ref, hv_ref, hl_ref,
                 mu_ref, vec_ref, w1_ref, w2_ref, e_ref, vf_ref,
                 r_o, lw_o, k_o, v_o, a_o, b_o, g_o):
    width, d_w, d_g = dims
    i = pl.program_id(1)

    def shifted(z_ref, h_ref, mu):
        z = z_ref[0]
        prev = jnp.where(i == 0, 0.0, h_ref[0, SHIFT_HALO - 1:SHIFT_HALO, :])
        zp = pltpu.roll(z, 1, 0)
        row = lax.broadcasted_iota(jnp.int32, z.shape, 0)
        zp = jnp.where(row == 0, prev, zp)
        return z + (zp - z) * mu

    w0, a0, v0 = vec_ref[0:1, :], vec_ref[1:2, :], vec_ref[2:3, :]
    k_k, k_a = vec_ref[3:4, :], vec_ref[4:5, :]

    lz = shifted(zl_ref, hl_ref, mu_ref[:, 3 * width:])
    s1 = lz[:, :LANE]
    lane1 = lax.broadcasted_iota(jnp.int32, s1.shape, 1)
    act1 = jnp.where(lane1 < d_w, jnp.tanh(s1), s1)
    lo1 = _dot(act1.astype(BF16), w1_ref[...])
    s2 = lz[:, LANE:LANE + w2_ref.shape[0]]
    lane2 = lax.broadcasted_iota(jnp.int32, s2.shape, 1)
    act2 = jnp.where(lane2 < d_g, jax.nn.sigmoid(s2), s2)
    lo2 = _dot(act2.astype(BF16), w2_ref[...])

    u = w0 + lo1[:, :width]
    w = -(jnp.maximum(-u, 0.0) + jnp.log1p(jnp.exp(-jnp.abs(u)))) - 0.5
    lw_o[0] = -jnp.exp(w)
    a = jax.nn.sigmoid(a0 + lo1[:, width:])
    g_o[0] = lo2[:, :width]

    r_o[0] = shifted(zr_ref, hr_ref, mu_ref[:, :width])
    k = shifted(zk_ref, hk_ref, mu_ref[:, width:2 * width])
    v = shifted(zv_ref, hv_ref, mu_ref[:, 2 * width:3 * width])
    if not first:
        v = v + (vf_ref[0] - v) * jax.nn.sigmoid(v0 + lo2[:, width:])
    v_o[0] = v

    x = k * k_k
    kk = x / jnp.maximum(jnp.sqrt(_segsum(x * x, e_ref[...])), 1e-12)
    k_o[0] = k * (1.0 + (a - 1.0) * k_a)
    a_o[0] = -kk
    b_o[0] = kk * a


def _prep(z, mu, vecs, w1, w2, e, v_first, layer, first, width, d_w, d_g, tt):
    b, t, zw = z.shape
    tt = min(tt, t)
    lw = zw - 4 * width
    hb = tt // SHIFT_HALO
    lcol = 4 * width // lw
    assert 4 * width % lw == 0

    def main(col, w):
        return pl.BlockSpec((1, tt, w), lambda bi, i: (bi, i, col))

    def halo(col, w):
        return pl.BlockSpec((1, SHIFT_HALO, w), lambda bi, i: (bi, jnp.maximum(i * hb - 1, 0), col))

    def whole(a):
        return pl.BlockSpec((None,) + a.shape[1:], lambda bi, i: (layer,) + (0,) * (a.ndim - 1))

    out = jax.ShapeDtypeStruct((b, t, width), F32)
    return pl.pallas_call(
        functools.partial(_prep_kernel, first, (width, d_w, d_g)),
        grid=(b, t // tt),
        in_specs=[
            main(1, width), main(2, width), main(3, width), main(lcol, lw),
            halo(1, width), halo(2, width), halo(3, width), halo(lcol, lw),
            whole(mu), whole(vecs), whole(w1), whole(w2),
            pl.BlockSpec(e.shape, lambda bi, i: (0, 0)),
            pl.BlockSpec((1, tt, width), lambda bi, i: (bi, i, 0)),
        ],
        out_specs=[pl.BlockSpec((1, tt, width), lambda bi, i: (bi, i, 0))] * 7,
        out_shape=[out] * 7,
        compiler_params=_params("parallel", "parallel"),
        name="rwkv_prep",
    )(z, z, z, z, z, z, z, z, mu, vecs, w1, w2, e, v_first)


def _wkv_kernel(r_ref, lw_ref, k_ref, v_ref, a_ref, b_ref, y_ref, s_ref):
    hg = WKV_HEADS_PER_GROUP
    c = pl.program_id(1)
    nrow, L, C = r_ref.shape
    W, GL = hg * HEAD_SIZE, hg * L
    ng = C // W

    @pl.when(c == 0)
    def _():
        s_ref[...] = jnp.zeros_like(s_ref)

    ti = lax.broadcasted_iota(jnp.int32, (L, L), 0)
    tj = lax.broadcasted_iota(jnp.int32, (L, L), 1)
    tri = jnp.where(tj <= ti, 1.0, 0.0).astype(BF16)

    def decayed_operands(b):
        lw = lw_ref[b]
        h1 = lw.astype(BF16)
        r1 = lw - h1.astype(F32)
        h2 = r1.astype(BF16)
        h3 = (r1 - h2.astype(F32)).astype(BF16)
        cum = _dot(tri, h1) + _dot(tri, h2) + _dot(tri, h3)
        tot = cum[L - 1:L, :]
        e_inv = jnp.exp(-cum)
        e_end = jnp.exp(tot - cum)
        return dict(
            wl=jnp.exp(tot), at=a_ref[b] * jnp.exp(cum - lw), rt=r_ref[b] * jnp.exp(cum),
            bt=b_ref[b] * e_inv, kt=k_ref[b] * e_inv,
            bp=(b_ref[b] * e_end).astype(BF16), kp=(k_ref[b] * e_end).astype(BF16), vv=v_ref[b])

    rows = [decayed_operands(b) for b in range(nrow)]

    lane = lax.broadcasted_iota(jnp.int32, (L, W), 1)
    head_mask = [(lane >= h * HEAD_SIZE) & (lane < (h + 1) * HEAD_SIZE) for h in range(hg)]
    row2 = lax.broadcasted_iota(jnp.int32, (L, 2 * GL), 0)
    col2 = lax.broadcasted_iota(jnp.int32, (L, 2 * GL), 1) & (L - 1)
    strict = col2 < row2
    incl = col2 <= row2
    colg = lax.broadcasted_iota(jnp.int32, (L, GL), 1)
    rowg = lax.broadcasted_iota(jnp.int32, (L, GL), 0)
    eye = (colg & (L - 1)) == rowg
    blk_mask = [(colg >= h * L) & (colg < (h + 1) * L) for h in range(hg)]
    sr = lax.broadcasted_iota(jnp.int32, (W, W), 0)
    sc = lax.broadcasted_iota(jnp.int32, (W, W), 1)
    same_head = (sr // HEAD_SIZE) == (sc // HEAD_SIZE)

    def stack_heads(x):
        return jnp.concatenate([jnp.where(m, x, 0.0).astype(BF16) for m in head_mask], axis=0)

    def block_diag(q):
        return jnp.concatenate([jnp.where(m, q, 0.0).astype(BF16) for m in blk_mask], axis=0)

    chains = [(b, g) for b in range(nrow) for g in range(ng)]
    ids = range(len(chains))
    op = lambda name, i: rows[chains[i][0]][name][:, chains[i][1] * W:(chains[i][1] + 1) * W]
    lhs = [jnp.concatenate([op("at", i).astype(BF16), op("rt", i).astype(BF16)], axis=0) for i in ids]
    gm = [_dot_nt(lhs[i], jnp.concatenate([stack_heads(op("bt", i)), stack_heads(op("kt", i))],
                                          axis=0)) for i in ids]
    s0 = [s_ref[i] for i in ids]
    as0 = [_dot_nt(lhs[i], s0[i].astype(BF16)) for i in ids]
    pa = [jnp.where(strict, gm[i][:L], 0.0) for i in ids]
    pr = [jnp.where(incl, gm[i][L:], 0.0) for i in ids]
    vm = [stack_heads(op("vv", i)) for i in ids]

    tm = [jnp.where(eye, 1.0, pa[i][:, :GL]) for i in ids]
    q = [_dot(pa[i][:, :GL].astype(BF16), block_diag(pa[i][:, :GL])) for i in ids]
    x1 = [as0[i][:L] + _dot(pa[i][:, GL:].astype(BF16), vm[i]) for i in ids]
    levels = L.bit_length() - 2
    for j in range(levels):
        if j + 1 < levels:
            res = [_dot(jnp.concatenate([q[i].astype(BF16), tm[i].astype(BF16)], axis=0),
                        block_diag(q[i])) for i in ids]
            q = [res[i][:L] for i in ids]
            tm = [tm[i] + res[i][L:] for i in ids]
        else:
            tm = [tm[i] + _dot(tm[i].astype(BF16), block_diag(q[i])) for i in ids]

    u = [_dot(tm[i].astype(BF16), stack_heads(x1[i])) for i in ids]
    for i, (b, g) in enumerate(chains):
        uvm = jnp.concatenate([stack_heads(u[i]), vm[i]], axis=0)
        y_ref[b, :, g * W:(g + 1) * W] = as0[i][L:] + _dot(pr[i].astype(BF16), uvm)
    for i in ids:
        uv = jnp.concatenate([u[i].astype(BF16), op("vv", i).astype(BF16)], axis=0)
        bk = jnp.concatenate([op("bp", i), op("kp", i)], axis=0)
        ds = _dot_tn(uv, bk)
        s_ref[i] = s0[i] * op("wl", i) + jnp.where(same_head, ds, 0.0)


def _wkv(r, lw, k, v, a, b):
    bsz, t, c = r.shape
    L = min(WKV_CHUNK, t)
    w = WKV_HEADS_PER_GROUP * HEAD_SIZE
    assert L & (L - 1) == 0 and t % L == 0 and c % w == 0
    nrow = WKV_ROWS_PER_STEP if bsz % WKV_ROWS_PER_STEP == 0 else 1
    spec = pl.BlockSpec((nrow, L, c), lambda bi, ci: (bi, ci, 0))
    return pl.pallas_call(
        _wkv_kernel,
        grid=(bsz // nrow, t // L),
        in_specs=[spec] * 6,
        out_specs=spec,
        out_shape=jax.ShapeDtypeStruct((bsz, t, c), F32),
        scratch_shapes=[pltpu.VMEM((nrow * (c // w), w, w), F32)],
        compiler_params=_params("parallel", "arbitrary"),
        name="wkv_scan",
    )(r, lw, k, v, a, b)


def _post_kernel(y_ref, r_ref, k_ref, v_ref, g_ref, vec_ref, e_ref, o_ref):
    e = e_ref[...]
    gn_g, gn_b, r_k = vec_ref[0:1, :], vec_ref[1:2, :], vec_ref[2:3, :]
    inv_n = 1.0 / HEAD_SIZE
    y = y_ref[0]
    d = y - _segsum(y, e) * inv_n
    var = _segsum(d * d, e) * inv_n
    yn = d * lax.rsqrt(var + GN_EPS) * gn_g + gn_b
    bonus = _segsum(r_ref[0] * k_ref[0] * r_k, e)
    o_ref[0] = ((yn + bonus * v_ref[0]) * g_ref[0]).astype(o_ref.dtype)


def _post(y, r, k, v, g, vecs, e, layer, tt):
    b, t, c = y.shape
    tt = min(tt, t)
    spec = pl.BlockSpec((1, tt, c), lambda bi, i: (bi, i, 0))
    return pl.pallas_call(
        _post_kernel,
        grid=(b, t // tt),
        in_specs=[spec] * 5 + [
            pl.BlockSpec((None,) + vecs.shape[1:], lambda bi, i: (layer, 0, 0)),
            pl.BlockSpec(e.shape, lambda bi, i: (0, 0)),
        ],
        out_specs=spec,
        out_shape=jax.ShapeDtypeStruct((b, t, c), BF16),
        compiler_params=_params("parallel", "parallel"),
        name="rwkv_post",
    )(y, r, k, v, g, vecs, e)


def kernel(x, p, attn_norm, w_in, mu_shift, w_vres_dn, mu_vres, v0, v_up, pool_w, pool_scale,
           w0, w_up, a0, a_up, g_up, k_k, k_a, r_k, gn_g, gn_b, w_out, mlp_norm, w_ffn_up,
           w_ffn_down, ple_norm, w_ple_gate, w_ple_proj, final_norm):
    bsz, t, d = x.shape
    depth = w_in.shape[0]
    width = w0.shape[1]
    pool_width = pool_scale.shape[1]
    d_w, d_a, d_g, d_v = w_up.shape[1], a_up.shape[1], g_up.shape[1], v_up.shape[1]
    d_ple = p.shape[-1]
    m = bsz * t
    assert pool_width == width and d_w + d_a == LANE and width % SEG_SLAB == 0
    lora_w = 2 * SEG_SLAB
    w2_rows = lora_w - LANE - LANE
    assert d_g + d_v <= w2_rows
    zw = pool_width + 3 * width + lora_w

    n_in = w_in.shape[2]
    vres = jnp.concatenate([jnp.zeros((1, d, d_v), F32), w_vres_dn], axis=0)
    w_in_x = jnp.concatenate(
        [w_in, vres, jnp.zeros((depth, d, zw - n_in - d_v), F32)], axis=2).astype(BF16)
    mu_vres_x = jnp.concatenate([jnp.zeros((1, d_v), F32), mu_vres], axis=0)
    mu_x = jnp.concatenate(
        [mu_shift, mu_vres_x, jnp.zeros((depth, zw - n_in - d_v), F32)], axis=1)[:, None, :]
    zeros_w = lambda rows: jnp.zeros((depth, rows, width), F32)
    w1 = jnp.concatenate([
        jnp.concatenate([w_up, zeros_w(d_w)], axis=2),
        jnp.concatenate([zeros_w(d_a), a_up], axis=2)], axis=1).astype(BF16)
    v_up_x = jnp.concatenate([jnp.zeros((1, d_v, width), F32), v_up], axis=0)
    w2 = jnp.concatenate([
        jnp.concatenate([g_up, zeros_w(d_g)], axis=2),
        jnp.concatenate([zeros_w(d_v), v_up_x], axis=2),
        jnp.zeros((depth, w2_rows - d_g - d_v, 2 * width), F32)], axis=1).astype(BF16)
    v0_x = jnp.concatenate([jnp.zeros((1, width), F32), v0], axis=0)
    pad3 = jnp.zeros((depth, 3, width), F32)
    prep_vecs = jnp.concatenate(
        [jnp.stack([w0, a0, v0_x, k_k, k_a], axis=1), pad3], axis=1)
    post_vecs = jnp.concatenate(
        [jnp.stack([gn_g, gn_b, r_k.reshape(depth, width)], axis=1), pad3,
         jnp.zeros((depth, 2, width), F32)], axis=1)
    seg = jnp.arange(SEG_SLAB) // HEAD_SIZE
    e = (seg[:, None] == seg[None, :]).astype(BF16)
    attn_g, mlp_g, ple_g = attn_norm[:, None, :], mlp_norm[:, None, :], ple_norm[:, None, :]
    pool_wb = pool_w.astype(BF16)
    pool_sc = pool_scale[:, None, :]
    w_out_b = w_out.astype(BF16).reshape(depth, 2, width, d)
    w_up_b, w_down_b = w_ffn_up.astype(BF16), w_ffn_down.astype(BF16)
    w_gate_b, w_proj_b = w_ple_gate.astype(BF16), w_ple_proj.astype(BF16)
    p2 = p.reshape(depth * m, d_ple)
    final_g = final_norm[None, :]

    xf = x.reshape(m, d)
    v_first = None
    for i in range(depth):
        z = _rms_matmul(xf, attn_g, w_in_x, i, tm=1024, tn=768).reshape(bsz, t, zw)
        pool_out = _pool(z, pool_wb, pool_sc, i, pool_width, tt=512)
        first = i == 0
        r, lw, k, v, a, b, g = _prep(z, mu_x, prep_vecs, w1, w2, e, z if first else v_first,
                                     i, first, width, d_w, d_g, tt=256)
        if first:
            v_first = v
        y = _wkv(r, lw, k, v, a, b)
        rwkv_out = _post(y, r, k, v, g, post_vecs, e, i, tt=512)
        xf = _outproj(xf, pool_out.reshape(m, pool_width), rwkv_out.reshape(m, width), w_out_b, i,
                      tm=1024, tn=1024)
        xf = _ffn(xf, mlp_g, w_up_b, w_down_b, i, tm=512, tf=1024)
        xf = _ple(xf, ple_g, w_gate_b, p2, w_proj_b, final_g, i, i == depth - 1, tm=512)
    return xf.reshape(bsz, t, d)
```

```python
import functools

import jax
import jax.numpy as jnp
from jax import lax
from jax.experimental import pallas as pl
from jax.experimental.pallas import tpu as pltpu

F32 = jnp.float32
BF16 = jnp.bfloat16

NORM_EPS = 1e-6
GN_EPS = 64e-5
HEAD_SIZE = 64
POOL_WINDOWS = (2, 4, 8, 16)
POOL_HALO = 16
SHIFT_HALO = 8
WKV_CHUNK = 64
WKV_HEADS_PER_GROUP = 2
WKV_ROWS_PER_STEP = 2
SEG_SLAB = 256
LANE = 128
VMEM_LIMIT = 52 * 1024 * 1024


def _dot(a, b):
    return jnp.dot(a, b, preferred_element_type=F32)


def _dot_nt(a, b):
    return lax.dot_general(a, b, (((1,), (1,)), ((), ())), preferred_element_type=F32)


def _dot_tn(a, b):
    return lax.dot_general(a, b, (((0,), (0,)), ((), ())), preferred_element_type=F32)


def _params(*sem):
    return pltpu.CompilerParams(dimension_semantics=sem, vmem_limit_bytes=VMEM_LIMIT)


def _rms_rows(x, g):
    ms = jnp.mean(x * x, axis=-1, keepdims=True)
    return x * lax.rsqrt(ms + NORM_EPS) * g


def _norm_into(h_ref, x_ref, g_ref, rows=256):
    rows = min(rows, x_ref.shape[0])
    n = x_ref.shape[0] // rows

    def body(c, carry):
        rs = pl.ds(pl.multiple_of(c * rows, rows), rows)
        h_ref[rs, :] = _rms_rows(x_ref[rs, :], g_ref[...]).astype(h_ref.dtype)
        return carry

    lax.fori_loop(0, n, body, 0)


def _segsum(x, e):
    outs = []
    for s in range(x.shape[1] // SEG_SLAB):
        xs = x[:, s * SEG_SLAB:(s + 1) * SEG_SLAB]
        hi = xs.astype(BF16)
        lo = (xs - hi.astype(F32)).astype(BF16)
        outs.append(_dot(hi, e) + _dot(lo, e))
    return jnp.concatenate(outs, axis=1)


def _row_parts(n_rows, parts=2):
    step = n_rows // parts
    return [slice(s * step, (s + 1) * step) for s in range(parts)]


def _rms_matmul_kernel(tn, x_ref, g_ref, w_ref, o_ref):
    g = g_ref[...]
    parts = _row_parts(x_ref.shape[0])
    hs = [_rms_rows(x_ref[rs, :], g).astype(BF16) for rs in parts]
    for rs, h in zip(parts, hs):
        for c in range(w_ref.shape[1] // tn):
            cs = slice(c * tn, (c + 1) * tn)
            o_ref[rs, cs] = _dot(h, w_ref[:, cs])


def _resident(block_shape, index_map):
    return pl.BlockSpec(block_shape, index_map, pipeline_mode=pl.Buffered(1))


def _rms_matmul(x, g, w, layer, tm, tn):
    m, d = x.shape
    n = w.shape[2]
    tm, tn = min(tm, m), min(tn, n)
    return pl.pallas_call(
        functools.partial(_rms_matmul_kernel, tn),
        grid=(m // tm,),
        in_specs=[
            pl.BlockSpec((tm, d), lambda i: (i, 0)),
            pl.BlockSpec((None, 1, d), lambda i: (layer, 0, 0)),
            _resident((None, d, n), lambda i: (layer, 0, 0)),
        ],
        out_specs=pl.BlockSpec((tm, n), lambda i: (i, 0)),
        out_shape=jax.ShapeDtypeStruct((m, n), F32),
        compiler_params=_params("parallel"),
        name="rms_matmul",
    )(x, g, w)


def _outproj_kernel(tn, x_ref, ma_ref, mb_ref, w_ref, o_ref):
    ka = ma_ref.shape[1]
    for c in range(x_ref.shape[1] // tn):
        cs = slice(c * tn, (c + 1) * tn)
        o_ref[:, cs] = (x_ref[:, cs] + _dot(ma_ref[...], w_ref[:ka, cs])
                        + _dot(mb_ref[...], w_ref[ka:, cs]))


def _outproj(x, mix_a, mix_b, w, layer, tm, tn):
    m, d = x.shape
    ka, kb = mix_a.shape[1], mix_b.shape[1]
    tm, tn = min(tm, m), min(tn, d)
    return pl.pallas_call(
        functools.partial(_outproj_kernel, tn),
        grid=(m // tm,),
        in_specs=[
            pl.BlockSpec((tm, d), lambda i: (i, 0)),
            pl.BlockSpec((tm, ka), lambda i: (i, 0)),
            pl.BlockSpec((tm, kb), lambda i: (i, 0)),
            _resident((None, ka + kb, d), lambda i: (layer, 0, 0)),
        ],
        out_specs=pl.BlockSpec((tm, d), lambda i: (i, 0)),
        out_shape=jax.ShapeDtypeStruct((m, d), F32),
        compiler_params=_params("parallel"),
        name="outproj",
    )(x, mix_a, mix_b, w)


def _ffn_kernel(x_ref, g_ref, wu_ref, wd_ref, o_ref, h_ref):
    @pl.when(pl.program_id(1) == 0)
    def _():
        _norm_into(h_ref, x_ref, g_ref)
        o_ref[...] = x_ref[...]

    a = _dot(h_ref[...], wu_ref[...])
    a = jnp.square(jnp.maximum(a, 0.0)).astype(BF16)
    o_ref[...] += _dot(a, wd_ref[...])


def _ffn(x, g, w_up, w_down, layer, tm, tf):
    m, d = x.shape
    f = w_up.shape[2]
    tm, tf = min(tm, m), min(tf, f)
    return pl.pallas_call(
        _ffn_kernel,
        grid=(m // tm, f // tf),
        in_specs=[
            pl.BlockSpec((tm, d), lambda i, j: (i, 0)),
            pl.BlockSpec((None, 1, d), lambda i, j: (layer, 0, 0)),
            pl.BlockSpec((None, d, tf), lambda i, j: (layer, 0, j)),
            pl.BlockSpec((None, tf, d), lambda i, j: (layer, j, 0)),
        ],
        out_specs=pl.BlockSpec((tm, d), lambda i, j: (i, 0)),
        out_shape=jax.ShapeDtypeStruct((m, d), F32),
        scratch_shapes=[pltpu.VMEM((tm, d), BF16)],
        compiler_params=_params("parallel", "arbitrary"),
        name="ffn",
    )(x, g, w_up, w_down)


def _ple_kernel(final, tn, x_ref, g_ref, wg_ref, p_ref, wp_ref, fg_ref, o_ref):
    g = g_ref[...]
    parts = _row_parts(x_ref.shape[0])
    hs = [_rms_rows(x_ref[rs, :], g).astype(BF16) for rs in parts]
    for rs, h in zip(parts, hs):
        pb = p_ref[rs, :].astype(BF16)
        for c in range(x_ref.shape[1] // tn):
            cs = slice(c * tn, (c + 1) * tn)
            gate = jax.nn.sigmoid(_dot(h, wg_ref[:, cs]))
            o_ref[rs, cs] = x_ref[rs, cs] + gate * _dot(pb, wp_ref[:, cs])
    if final:
        _norm_into(o_ref, o_ref, fg_ref)


def _ple(x, g, w_gate, p, w_proj, final_g, layer, final, tm):
    m, d = x.shape
    dp = p.shape[1]
    tm = min(tm, m)
    nblk = m // tm
    return pl.pallas_call(
        functools.partial(_ple_kernel, final, min(512, d)),
        grid=(nblk,),
        in_specs=[
            pl.BlockSpec((tm, d), lambda i: (i, 0)),
            pl.BlockSpec((None, 1, d), lambda i: (layer, 0, 0)),
            _resident((None, d, d), lambda i: (layer, 0, 0)),
            pl.BlockSpec((tm, dp), lambda i: (layer * nblk + i, 0)),
            _resident((None, dp, d), lambda i: (layer, 0, 0)),
            pl.BlockSpec((1, d), lambda i: (0, 0)),
        ],
        out_specs=pl.BlockSpec((tm, d), lambda i: (i, 0)),
        out_shape=jax.ShapeDtypeStruct((m, d), F32),
        compiler_params=_params("parallel"),
        name="ple",
    )(x, g, w_gate, p, w_proj, final_g)


def _pool_kernel(z_ref, halo_ref, pw_ref, ps_ref, o_ref):
    i = pl.program_id(1)
    tt, width = z_ref.shape[1], z_ref.shape[2]
    pg = width // len(POOL_WINDOWS)
    u = z_ref[0]
    halo = jnp.where(i == 0, 0.0, halo_ref[0])
    cur = jnp.concatenate([halo, u], axis=0)
    span, lane0 = 1, 0
    t = i * tt + lax.broadcasted_iota(jnp.int32, (tt, pg), 0)
    for gi, win in enumerate(POOL_WINDOWS):
        cur = cur[:, gi * pg - lane0:]
        lane0 = gi * pg
        while span < win:
            cur = cur + pltpu.roll(cur, span, 0)
            span *= 2
        cols = slice(gi * pg, (gi + 1) * pg)
        cnt = jnp.minimum(t + 1, win).astype(F32)
        d = cur[POOL_HALO:, :pg] / cnt - u[:, cols]
        o = _dot(d.astype(BF16), pw_ref[gi]) * ps_ref[:, cols]
        o_ref[0, :, cols] = o.astype(o_ref.dtype)


def _pool(z, pool_w, pool_scale, layer, width, tt):
    b, t, _ = z.shape
    g, pg, _ = pool_w.shape[1:]
    tt = min(tt, t)
    assert all(w & (w - 1) == 0 for w in POOL_WINDOWS) and max(POOL_WINDOWS) <= POOL_HALO
    assert list(POOL_WINDOWS) == sorted(POOL_WINDOWS) and g == len(POOL_WINDOWS) and g * pg == width
    hb = tt // POOL_HALO
    return pl.pallas_call(
        _pool_kernel,
        grid=(b, t // tt),
        in_specs=[
            pl.BlockSpec((1, tt, width), lambda bi, i: (bi, i, 0)),
            pl.BlockSpec((1, POOL_HALO, width), lambda bi, i: (bi, jnp.maximum(i * hb - 1, 0), 0)),
            pl.BlockSpec((None, g, pg, pg), lambda bi, i: (layer, 0, 0, 0)),
            pl.BlockSpec((None, 1, width), lambda bi, i: (layer, 0, 0)),
        ],
        out_specs=pl.BlockSpec((1, tt, width), lambda bi, i: (bi, i, 0)),
        out_shape=jax.ShapeDtypeStruct((b, t, width), BF16),
        compiler_params=_params("parallel", "parallel"),
        name="pool",
    )(z, z, pool_w, pool_scale)


def _prep_kernel(first, dims, zr_ref, zk_ref, zv_ref, zl_ref, hr_ref, hk_ref, hv_ref, hl_ref,
                 mu_ref, vec_ref, w1_ref, w2_ref, e_ref, vf_ref,
                 r_o, lw_o, k_o, v_o, a_o, b_o, g_o):
    width, d_w, d_g = dims
    i = pl.program_id(1)

    def shifted(z_ref, h_ref, mu):
        z = z_ref[0]
        prev = jnp.where(i == 0, 0.0, h_ref[0, SHIFT_HALO - 1:SHIFT_HALO, :])
        zp = pltpu.roll(z, 1, 0)
        row = lax.broadcasted_iota(jnp.int32, z.shape, 0)
        zp = jnp.where(row == 0, prev, zp)
        return z + (zp - z) * mu

    w0, a0, v0 = vec_ref[0:1, :], vec_ref[1:2, :], vec_ref[2:3, :]
    k_k, k_a = vec_ref[3:4, :], vec_ref[4:5, :]

    lz = shifted(zl_ref, hl_ref, mu_ref[:, 3 * width:])
    s1 = lz[:, :LANE]
    lane1 = lax.broadcasted_iota(jnp.int32, s1.shape, 1)
    act1 = jnp.where(lane1 < d_w, jnp.tanh(s1), s1)
    lo1 = _dot(act1.astype(BF16), w1_ref[...])
    s2 = lz[:, LANE:LANE + w2_ref.shape[0]]
    lane2 = lax.broadcasted_iota(jnp.int32, s2.shape, 1)
    act2 = jnp.where(lane2 < d_g, jax.nn.sigmoid(s2), s2)
    lo2 = _dot(act2.astype(BF16), w2_ref[...])

    u = w0 + lo1[:, :width]
    w = -(jnp.maximum(-u, 0.0) + jnp.log1p(jnp.exp(-jnp.abs(u)))) - 0.5
    lw_o[0] = -jnp.exp(w)
    a = jax.nn.sigmoid(a0 + lo1[:, width:])
    g_o[0] = lo2[:, :width]

    r_o[0] = shifted(zr_ref, hr_ref, mu_ref[:, :width])
    k = shifted(zk_ref, hk_ref, mu_ref[:, width:2 * width])
    v = shifted(zv_ref, hv_ref, mu_ref[:, 2 * width:3 * width])
    if not first:
        v = v + (vf_ref[0] - v) * jax.nn.sigmoid(v0 + lo2[:, width:])
    v_o[0] = v

    x = k * k_k
    kk = x / jnp.maximum(jnp.sqrt(_segsum(x * x, e_ref[...])), 1e-12)
    k_o[0] = k * (1.0 + (a - 1.0) * k_a)
    a_o[0] = -kk
    b_o[0] = kk * a


def _prep(z, mu, vecs, w1, w2, e, v_first, layer, first, width, d_w, d_g, tt):
    b, t, zw = z.shape
    tt = min(tt, t)
    lw = zw - 4 * width
    hb = tt // SHIFT_HALO
    lcol = 4 * width // lw
    assert 4 * width % lw == 0

    def main(col, w):
        return pl.BlockSpec((1, tt, w), lambda bi, i: (bi, i, col))

    def halo(col, w):
        return pl.BlockSpec((1, SHIFT_HALO, w), lambda bi, i: (bi, jnp.maximum(i * hb - 1, 0), col))

    def whole(a):
        return pl.BlockSpec((None,) + a.shape[1:], lambda bi, i: (layer,) + (0,) * (a.ndim - 1))

    out = jax.ShapeDtypeStruct((b, t, width), F32)
    return pl.pallas_call(
        functools.partial(_prep_kernel, first, (width, d_w, d_g)),
        grid=(b, t // tt),
        in_specs=[
            main(1, width), main(2, width), main(3, width), main(lcol, lw),
            halo(1, width), halo(2, width), halo(3, width), halo(lcol, lw),
            whole(mu), whole(vecs), whole(w1), whole(w2),
            pl.BlockSpec(e.shape, lambda bi, i: (0, 0)),
            pl.BlockSpec((1, tt, width), lambda bi, i: (bi, i, 0)),
        ],
        out_specs=[pl.BlockSpec((1, tt, width), lambda bi, i: (bi, i, 0))] * 7,
        out_shape=[out] * 7,
        compiler_params=_params("parallel", "parallel"),
        name="rwkv_prep",
    )(z, z, z, z, z, z, z, z, mu, vecs, w1, w2, e, v_first)


def _wkv_kernel(r_ref, lw_ref, k_ref, v_ref, a_ref, b_ref, y_ref, s_ref):
    hg = WKV_HEADS_PER_GROUP
    c = pl.program_id(1)
    nrow, L, C = r_ref.shape
    W, GL = hg * HEAD_SIZE, hg * L
    ng = C // W

    @pl.when(c == 0)
    def _():
        s_ref[...] = jnp.zeros_like(s_ref)

    ti = lax.broadcasted_iota(jnp.int32, (L, L), 0)
    tj = lax.broadcasted_iota(jnp.int32, (L, L), 1)
    tri = jnp.where(tj <= ti, 1.0, 0.0).astype(BF16)

    def decayed_operands(b):
        lw = lw_ref[b]
        h1 = lw.astype(BF16)
        r1 = lw - h1.astype(F32)
        h2 = r1.astype(BF16)
        h3 = (r1 - h2.astype(F32)).astype(BF16)
        cum = _dot(tri, h1) + _dot(tri, h2) + _dot(tri, h3)
        tot = cum[L - 1:L, :]
        e_inv = jnp.exp(-cum)
        e_end = jnp.exp(tot - cum)
        return dict(
            wl=jnp.exp(tot), at=a_ref[b] * jnp.exp(cum - lw), rt=r_ref[b] * jnp.exp(cum),
            bt=b_ref[b] * e_inv, kt=k_ref[b] * e_inv,
            bp=(b_ref[b] * e_end).astype(BF16), kp=(k_ref[b] * e_end).astype(BF16), vv=v_ref[b])

    rows = [decayed_operands(b) for b in range(nrow)]

    lane = lax.broadcasted_iota(jnp.int32, (L, W), 1)
    head_mask = [(lane >= h * HEAD_SIZE) & (lane < (h + 1) * HEAD_SIZE) for h in range(hg)]
    row2 = lax.broadcasted_iota(jnp.int32, (L, 2 * GL), 0)
    col2 = lax.broadcasted_iota(jnp.int32, (L, 2 * GL), 1) & (L - 1)
    strict = col2 < row2
    incl = col2 <= row2
    colg = lax.broadcasted_iota(jnp.int32, (L, GL), 1)
    rowg = lax.broadcasted_iota(jnp.int32, (L, GL), 0)
    eye = (colg & (L - 1)) == rowg
    blk_mask = [(colg >= h * L) & (colg < (h + 1) * L) for h in range(hg)]
    sr = lax.broadcasted_iota(jnp.int32, (W, W), 0)
    sc = lax.broadcasted_iota(jnp.int32, (W, W), 1)
    same_head = (sr // HEAD_SIZE) == (sc // HEAD_SIZE)

    def stack_heads(x):
        return jnp.concatenate([jnp.where(m, x, 0.0).astype(BF16) for m in head_mask], axis=0)

    def block_diag(q):
        return jnp.concatenate([jnp.where(m, q, 0.0).astype(BF16) for m in blk_mask], axis=0)

    chains = [(b, g) for b in range(nrow) for g in range(ng)]
    ids = range(len(chains))
    op = lambda name, i: rows[chains[i][0]][name][:, chains[i][1] * W:(chains[i][1] + 1) * W]
    lhs = [jnp.concatenate([op("at", i).astype(BF16), op("rt", i).astype(BF16)], axis=0) for i in ids]
    gm = [_dot_nt(lhs[i], jnp.concatenate([stack_heads(op("bt", i)), stack_heads(op("kt", i))],
                                          axis=0)) for i in ids]
    s0 = [s_ref[i] for i in ids]
    as0 = [_dot_nt(lhs[i], s0[i].astype(BF16)) for i in ids]
    pa = [jnp.where(strict, gm[i][:L], 0.0) for i in ids]
    pr = [jnp.where(incl, gm[i][L:], 0.0) for i in ids]
    vm = [stack_heads(op("vv", i)) for i in ids]

    tm = [jnp.where(eye, 1.0, pa[i][:, :GL]) for i in ids]
    q = [_dot(pa[i][:, :GL].astype(BF16), block_diag(pa[i][:, :GL])) for i in ids]
    x1 = [as0[i][:L] + _dot(pa[i][:, GL:].astype(BF16), vm[i]) for i in ids]
    levels = L.bit_length() - 2
    for j in range(levels):
        if j + 1 < levels:
            res = [_dot(jnp.concatenate([q[i].astype(BF16), tm[i].astype(BF16)], axis=0),
                        block_diag(q[i])) for i in ids]
            q = [res[i][:L] for i in ids]
            tm = [tm[i] + res[i][L:] for i in ids]
        else:
            tm = [tm[i] + _dot(tm[i].astype(BF16), block_diag(q[i])) for i in ids]

    u = [_dot(tm[i].astype(BF16), stack_heads(x1[i])) for i in ids]
    for i, (b, g) in enumerate(chains):
        uvm = jnp.concatenate([stack_heads(u[i]), vm[i]], axis=0)
        y_ref[b, :, g * W:(g + 1) * W] = as0[i][L:] + _dot(pr[i].astype(BF16), uvm)
    for i in ids:
        uv = jnp.concatenate([u[i].astype(BF16), op("vv", i).astype(BF16)], axis=0)
        bk = jnp.concatenate([op("bp", i), op("kp", i)], axis=0)
        ds = _dot_tn(uv, bk)
        s_ref[i] = s0[i] * op("wl", i) + jnp.where(same_head, ds, 0.0)


def _wkv(r, lw, k, v, a, b):
    bsz, t, c = r.shape
    L = min(WKV_CHUNK, t)
    w = WKV_HEADS_PER_GROUP * HEAD_SIZE
    assert L & (L - 1) == 0 and t % L == 0 and c % w == 0
    nrow = WKV_ROWS_PER_STEP if bsz % WKV_ROWS_PER_STEP == 0 else 1
    spec = pl.BlockSpec((nrow, L, c), lambda bi, ci: (bi, ci, 0))
    return pl.pallas_call(
        _wkv_kernel,
        grid=(bsz // nrow, t // L),
        in_specs=[spec] * 6,
        out_specs=spec,
        out_shape=jax.ShapeDtypeStruct((bsz, t, c), F32),
        scratch_shapes=[pltpu.VMEM((nrow * (c // w), w, w), F32)],
        compiler_params=_params("parallel", "arbitrary"),
        name="wkv_scan",
    )(r, lw, k, v, a, b)


def _post_kernel(y_ref, r_ref, k_ref, v_ref, g_ref, vec_ref, e_ref, o_ref):
    e = e_ref[...]
    gn_g, gn_b, r_k = vec_ref[0:1, :], vec_ref[1:2, :], vec_ref[2:3, :]
    inv_n = 1.0 / HEAD_SIZE
    y = y_ref[0]
    d = y - _segsum(y, e) * inv_n
    var = _segsum(d * d, e) * inv_n
    yn = d * lax.rsqrt(var + GN_EPS) * gn_g + gn_b
    bonus = _segsum(r_ref[0] * k_ref[0] * r_k, e)
    o_ref[0] = ((yn + bonus * v_ref[0]) * g_ref[0]).astype(o_ref.dtype)


def _post(y, r, k, v, g, vecs, e, layer, tt):
    b, t, c = y.shape
    tt = min(tt, t)
    spec = pl.BlockSpec((1, tt, c), lambda bi, i: (bi, i, 0))
    return pl.pallas_call(
        _post_kernel,
        grid=(b, t // tt),
        in_specs=[spec] * 5 + [
            pl.BlockSpec((None,) + vecs.shape[1:], lambda bi, i: (layer, 0, 0)),
            pl.BlockSpec(e.shape, lambda bi, i: (0, 0)),
        ],
        out_specs=spec,
        out_shape=jax.ShapeDtypeStruct((b, t, c), BF16),
        compiler_params=_params("parallel", "parallel"),
        name="rwkv_post",
    )(y, r, k, v, g, vecs, e)


def kernel(x, p, attn_norm, w_in, mu_shift, w_vres_dn, mu_vres, v0, v_up, pool_w, pool_scale,
           w0, w_up, a0, a_up, g_up, k_k, k_a, r_k, gn_g, gn_b, w_out, mlp_norm, w_ffn_up,
           w_ffn_down, ple_norm, w_ple_gate, w_ple_proj, final_norm):
    bsz, t, d = x.shape
    depth = w_in.shape[0]
    width = w0.shape[1]
    pool_width = pool_scale.shape[1]
    d_w, d_a, d_g, d_v = w_up.shape[1], a_up.shape[1], g_up.shape[1], v_up.shape[1]
    d_ple = p.shape[-1]
    m = bsz * t
    assert pool_width == width and d_w + d_a == LANE and width % SEG_SLAB == 0
    lora_w = 2 * SEG_SLAB
    w2_rows = lora_w - LANE - LANE
    assert d_g + d_v <= w2_rows
    zw = pool_width + 3 * width + lora_w

    n_in = w_in.shape[2]
    vres = jnp.concatenate([jnp.zeros((1, d, d_v), F32), w_vres_dn], axis=0)
    w_in_x = jnp.concatenate(
        [w_in, vres, jnp.zeros((depth, d, zw - n_in - d_v), F32)], axis=2).astype(BF16)
    mu_vres_x = jnp.concatenate([jnp.zeros((1, d_v), F32), mu_vres], axis=0)
    mu_x = jnp.concatenate(
        [mu_shift, mu_vres_x, jnp.zeros((depth, zw - n_in - d_v), F32)], axis=1)[:, None, :]
    zeros_w = lambda rows: jnp.zeros((depth, rows, width), F32)
    w1 = jnp.concatenate([
        jnp.concatenate([w_up, zeros_w(d_w)], axis=2),
        jnp.concatenate([zeros_w(d_a), a_up], axis=2)], axis=1).astype(BF16)
    v_up_x = jnp.concatenate([jnp.zeros((1, d_v, width), F32), v_up], axis=0)
    w2 = jnp.concatenate([
        jnp.concatenate([g_up, zeros_w(d_g)], axis=2),
        jnp.concatenate([zeros_w(d_v), v_up_x], axis=2),
        jnp.zeros((depth, w2_rows - d_g - d_v, 2 * width), F32)], axis=1).astype(BF16)
    v0_x = jnp.concatenate([jnp.zeros((1, width), F32), v0], axis=0)
    pad3 = jnp.zeros((depth, 3, width), F32)
    prep_vecs = jnp.concatenate(
        [jnp.stack([w0, a0, v0_x, k_k, k_a], axis=1), pad3], axis=1)
    post_vecs = jnp.concatenate(
        [jnp.stack([gn_g, gn_b, r_k.reshape(depth, width)], axis=1), pad3,
         jnp.zeros((depth, 2, width), F32)], axis=1)
    seg = jnp.arange(SEG_SLAB) // HEAD_SIZE
    e = (seg[:, None] == seg[None, :]).astype(BF16)
    attn_g, mlp_g, ple_g = attn_norm[:, None, :], mlp_norm[:, None, :], ple_norm[:, None, :]
    pool_wb = pool_w.astype(BF16)
    pool_sc = pool_scale[:, None, :]
    w_out_b = w_out.astype(BF16)
    w_up_b, w_down_b = w_ffn_up.astype(BF16), w_ffn_down.astype(BF16)
    w_gate_b, w_proj_b = w_ple_gate.astype(BF16), w_ple_proj.astype(BF16)
    p2 = p.reshape(depth * m, d_ple)
    final_g = final_norm[None, :]

    xf = x.reshape(m, d)
    v_first = None
    for i in range(depth):
        z = _rms_matmul(xf, attn_g, w_in_x, i, tm=512, tn=768).reshape(bsz, t, zw)
        pool_out = _pool(z, pool_wb, pool_sc, i, pool_width, tt=512)
        first = i == 0
        r, lw, k, v, a, b, g = _prep(z, mu_x, prep_vecs, w1, w2, e, z if first else v_first,
                                     i, first, width, d_w, d_g, tt=256)
        if first:
            v_first = v
        y = _wkv(r, lw, k, v, a, b)
        rwkv_out = _post(y, r, k, v, g, post_vecs, e, i, tt=512)
        xf = _outproj(xf, pool_out.reshape(m, pool_width), rwkv_out.reshape(m, width), w_out_b, i,
                      tm=512, tn=512)
        xf = _ffn(xf, mlp_g, w_up_b, w_down_b, i, tm=512, tf=1024)
        xf = _ple(xf, ple_g, w_gate_b, p2, w_proj_b, final_g, i, i == depth - 1, tm=512)
    return xf.reshape(bsz, t, d)
```

```python
import functools

import jax
import jax.numpy as jnp
from jax import lax
from jax.experimental import pallas as pl
from jax.experimental.pallas import tpu as pltpu

F32 = jnp.float32
BF16 = jnp.bfloat16

NORM_EPS = 1e-6
GN_EPS = 64e-5
EXP_NEG_HALF = 0.6065306597126334
HEAD_SIZE = 64
POOL_WINDOWS = (2, 4, 8, 16)
POOL_HALO = 16
SHIFT_HALO = 8
WKV_CHUNK = 64
WKV_HEADS_PER_GROUP = 2
WKV_ROWS_PER_STEP = 2
SEG_SLAB = 256
LANE = 128
VMEM_LIMIT = 52 * 1024 * 1024


def _dot(a, b):
    return jnp.dot(a, b, preferred_element_type=F32)


def _dot_nt(a, b):
    return lax.dot_general(a, b, (((1,), (1,)), ((), ())), preferred_element_type=F32)


def _dot_tn(a, b):
    return lax.dot_general(a, b, (((0,), (0,)), ((), ())), preferred_element_type=F32)


def _params(*sem):
    return pltpu.CompilerParams(dimension_semantics=sem, vmem_limit_bytes=VMEM_LIMIT)


def _rms_rows(x, g):
    ms = jnp.mean(x * x, axis=-1, keepdims=True)
    return x * lax.rsqrt(ms + NORM_EPS) * g


def _norm_into(h_ref, x_ref, g_ref, rows=256):
    rows = min(rows, x_ref.shape[0])
    n = x_ref.shape[0] // rows

    def body(c, carry):
        rs = pl.ds(pl.multiple_of(c * rows, rows), rows)
        h_ref[rs, :] = _rms_rows(x_ref[rs, :], g_ref[...]).astype(h_ref.dtype)
        return carry

    lax.fori_loop(0, n, body, 0)


def _segsum(x, e):
    outs = []
    for s in range(x.shape[1] // SEG_SLAB):
        xs = x[:, s * SEG_SLAB:(s + 1) * SEG_SLAB]
        hi = xs.astype(BF16)
        lo = (xs - hi.astype(F32)).astype(BF16)
        outs.append(_dot(hi, e) + _dot(lo, e))
    return jnp.concatenate(outs, axis=1)


def _row_parts(n_rows, parts=2):
    step = n_rows // parts
    return [slice(s * step, (s + 1) * step) for s in range(parts)]


def _rms_matmul_kernel(tn, x_ref, g_ref, w_ref, o_ref):
    g = g_ref[...]
    parts = _row_parts(x_ref.shape[0])
    hs = [_rms_rows(x_ref[rs, :], g).astype(BF16) for rs in parts]
    for rs, h in zip(parts, hs):
        for c in range(w_ref.shape[1] // tn):
            cs = slice(c * tn, (c + 1) * tn)
            o_ref[rs, cs] = _dot(h, w_ref[:, cs])


def _resident(block_shape, index_map):
    return pl.BlockSpec(block_shape, index_map, pipeline_mode=pl.Buffered(1))


def _rms_matmul(x, g, w, layer, tm, tn):
    m, d = x.shape
    n = w.shape[2]
    tm, tn = min(tm, m), min(tn, n)
    return pl.pallas_call(
        functools.partial(_rms_matmul_kernel, tn),
        grid=(m // tm,),
        in_specs=[
            pl.BlockSpec((tm, d), lambda i: (i, 0)),
            pl.BlockSpec((None, 1, d), lambda i: (layer, 0, 0)),
            _resident((None, d, n), lambda i: (layer, 0, 0)),
        ],
        out_specs=pl.BlockSpec((tm, n), lambda i: (i, 0)),
        out_shape=jax.ShapeDtypeStruct((m, n), F32),
        compiler_params=_params("parallel"),
        name="rms_matmul",
    )(x, g, w)


def _outproj_kernel(tn, x_ref, ma_ref, mb_ref, w_ref, o_ref):
    ka = ma_ref.shape[1]
    for c in range(x_ref.shape[1] // tn):
        cs = slice(c * tn, (c + 1) * tn)
        o_ref[:, cs] = (x_ref[:, cs] + _dot(ma_ref[...], w_ref[:ka, cs])
                        + _dot(mb_ref[...], w_ref[ka:, cs]))


def _outproj(x, mix_a, mix_b, w, layer, tm, tn):
    m, d = x.shape
    ka, kb = mix_a.shape[1], mix_b.shape[1]
    tm, tn = min(tm, m), min(tn, d)
    return pl.pallas_call(
        functools.partial(_outproj_kernel, tn),
        grid=(m // tm,),
        in_specs=[
            pl.BlockSpec((tm, d), lambda i: (i, 0)),
            pl.BlockSpec((tm, ka), lambda i: (i, 0)),
            pl.BlockSpec((tm, kb), lambda i: (i, 0)),
            _resident((None, ka + kb, d), lambda i: (layer, 0, 0)),
        ],
        out_specs=pl.BlockSpec((tm, d), lambda i: (i, 0)),
        out_shape=jax.ShapeDtypeStruct((m, d), F32),
        compiler_params=_params("parallel"),
        name="outproj",
    )(x, mix_a, mix_b, w)


def _ffn_kernel(x_ref, g_ref, wu_ref, wd_ref, o_ref, h_ref):
    @pl.when(pl.program_id(1) == 0)
    def _():
        _norm_into(h_ref, x_ref, g_ref)
        o_ref[...] = x_ref[...]

    a = _dot(h_ref[...], wu_ref[...])
    a = jnp.square(jnp.maximum(a, 0.0)).astype(BF16)
    o_ref[...] += _dot(a, wd_ref[...])


def _ffn(x, g, w_up, w_down, layer, tm, tf):
    m, d = x.shape
    f = w_up.shape[2]
    tm, tf = min(tm, m), min(tf, f)
    return pl.pallas_call(
        _ffn_kernel,
        grid=(m // tm, f // tf),
        in_specs=[
            pl.BlockSpec((tm, d), lambda i, j: (i, 0)),
            pl.BlockSpec((None, 1, d), lambda i, j: (layer, 0, 0)),
            pl.BlockSpec((None, d, tf), lambda i, j: (layer, 0, j)),
            pl.BlockSpec((None, tf, d), lambda i, j: (layer, j, 0)),
        ],
        out_specs=pl.BlockSpec((tm, d), lambda i, j: (i, 0)),
        out_shape=jax.ShapeDtypeStruct((m, d), F32),
        scratch_shapes=[pltpu.VMEM((tm, d), BF16)],
        compiler_params=_params("parallel", "arbitrary"),
        name="ffn",
    )(x, g, w_up, w_down)


def _ple_kernel(final, tn, x_ref, g_ref, wg_ref, p_ref, wp_ref, fg_ref, o_ref):
    g = g_ref[...]
    parts = _row_parts(x_ref.shape[0])
    hs = [_rms_rows(x_ref[rs, :], g).astype(BF16) for rs in parts]
    for rs, h in zip(parts, hs):
        pb = p_ref[rs, :].astype(BF16)
        for c in range(x_ref.shape[1] // tn):
            cs = slice(c * tn, (c + 1) * tn)
            gate = jax.nn.sigmoid(_dot(h, wg_ref[:, cs]))
            o_ref[rs, cs] = x_ref[rs, cs] + gate * _dot(pb, wp_ref[:, cs])
    if final:
        _norm_into(o_ref, o_ref, fg_ref)


def _ple(x, g, w_gate, p, w_proj, final_g, layer, final, tm):
    m, d = x.shape
    dp = p.shape[1]
    tm = min(tm, m)
    nblk = m // tm
    return pl.pallas_call(
        functools.partial(_ple_kernel, final, min(512, d)),
        grid=(nblk,),
        in_specs=[
            pl.BlockSpec((tm, d), lambda i: (i, 0)),
            pl.BlockSpec((None, 1, d), lambda i: (layer, 0, 0)),
            _resident((None, d, d), lambda i: (layer, 0, 0)),
            pl.BlockSpec((tm, dp), lambda i: (layer * nblk + i, 0)),
            _resident((None, dp, d), lambda i: (layer, 0, 0)),
            pl.BlockSpec((1, d), lambda i: (0, 0)),
        ],
        out_specs=pl.BlockSpec((tm, d), lambda i: (i, 0)),
        out_shape=jax.ShapeDtypeStruct((m, d), F32),
        compiler_params=_params("parallel"),
        name="ple",
    )(x, g, w_gate, p, w_proj, final_g)


def _pool_kernel(z_ref, halo_ref, pw_ref, ps_ref, o_ref):
    i = pl.program_id(1)
    tt, width = z_ref.shape[1], z_ref.shape[2]
    pg = width // len(POOL_WINDOWS)
    u = z_ref[0]
    halo = jnp.where(i == 0, 0.0, halo_ref[0])
    cur = jnp.concatenate([halo, u], axis=0)
    span, lane0 = 1, 0
    t = i * tt + lax.broadcasted_iota(jnp.int32, (tt, pg), 0)
    for gi, win in enumerate(POOL_WINDOWS):
        cur = cur[:, gi * pg - lane0:]
        lane0 = gi * pg
        while span < win:
            cur = cur + pltpu.roll(cur, span, 0)
            span *= 2
        cols = slice(gi * pg, (gi + 1) * pg)
        cnt = jnp.minimum(t + 1, win).astype(F32)
        d = cur[POOL_HALO:, :pg] / cnt - u[:, cols]
        o = _dot(d.astype(BF16), pw_ref[gi]) * ps_ref[:, cols]
        o_ref[0, :, cols] = o.astype(o_ref.dtype)


def _pool(z, pool_w, pool_scale, layer, width, tt):
    b, t, _ = z.shape
    g, pg, _ = pool_w.shape[1:]
    tt = min(tt, t)
    assert all(w & (w - 1) == 0 for w in POOL_WINDOWS) and max(POOL_WINDOWS) <= POOL_HALO
    assert list(POOL_WINDOWS) == sorted(POOL_WINDOWS) and g == len(POOL_WINDOWS) and g * pg == width
    hb = tt // POOL_HALO
    return pl.pallas_call(
        _pool_kernel,
        grid=(b, t // tt),
        in_specs=[
            pl.BlockSpec((1, tt, width), lambda bi, i: (bi, i, 0)),
            pl.BlockSpec((1, POOL_HALO, width), lambda bi, i: (bi, jnp.maximum(i * hb - 1, 0), 0)),
            pl.BlockSpec((None, g, pg, pg), lambda bi, i: (layer, 0, 0, 0)),
            pl.BlockSpec((None, 1, width), lambda bi, i: (layer, 0, 0)),
        ],
        out_specs=pl.BlockSpec((1, tt, width), lambda bi, i: (bi, i, 0)),
        out_shape=jax.ShapeDtypeStruct((b, t, width), BF16),
        compiler_params=_params("parallel", "parallel"),
        name="pool",
    )(z, z, pool_w, pool_scale)


def _prep_kernel(first, dims, zr_ref, zk_ref, zv_ref, zl_ref, hr_ref, hk_ref, hv_ref, hl_ref,
                 mu_ref, vec_ref, w1_ref, w2_ref, e_ref, tri_ref, vf_ref,
                 at_o, rt_o, bt_o, kt_o, bp_o, kp_o, vb_o, wl_o, bv_o, g_o, *v_o):
    width, d_w, d_g = dims
    i = pl.program_id(1)
    tt = zr_ref.shape[1]

    def shifted(z_ref, h_ref, mu):
        z = z_ref[0]
        prev = jnp.where(i == 0, 0.0, h_ref[0, SHIFT_HALO - 1:SHIFT_HALO, :])
        zp = pltpu.roll(z, 1, 0)
        row = lax.broadcasted_iota(jnp.int32, z.shape, 0)
        zp = jnp.where(row == 0, prev, zp)
        return z + (zp - z) * mu

    w0, a0, v0 = vec_ref[0:1, :], vec_ref[1:2, :], vec_ref[2:3, :]
    k_k, k_a, r_k = vec_ref[3:4, :], vec_ref[4:5, :], vec_ref[5:6, :]

    lz = shifted(zl_ref, hl_ref, mu_ref[:, 3 * width:])
    s1 = lz[:, :LANE]
    lane1 = lax.broadcasted_iota(jnp.int32, s1.shape, 1)
    act1 = jnp.where(lane1 < d_w, jnp.tanh(s1), s1)
    lo1 = _dot(act1.astype(BF16), w1_ref[...])
    s2 = lz[:, LANE:LANE + w2_ref.shape[0]]
    lane2 = lax.broadcasted_iota(jnp.int32, s2.shape, 1)
    act2 = jnp.where(lane2 < d_g, jax.nn.sigmoid(s2), s2)
    lo2 = _dot(act2.astype(BF16), w2_ref[...])
    g_o[0] = lo2[:, :width]

    lw = -EXP_NEG_HALF * jax.nn.sigmoid(w0 + lo1[:, :width])
    h1 = lw.astype(BF16)
    r1 = lw - h1.astype(F32)
    h2 = r1.astype(BF16)
    h3 = (r1 - h2.astype(F32)).astype(BF16)
    tri = tri_ref[...]
    sums = _dot(tri, h1) + _dot(tri, h2) + _dot(tri, h3)
    cum, tot = sums[:tt], sums[tt:]
    e_inv = jnp.exp(-cum)
    e_end = jnp.exp(tot - cum)
    for c in range(tt // WKV_CHUNK):
        wl_o[0, c] = jnp.exp(tot[c * WKV_CHUNK:c * WKV_CHUNK + 1, :])

    a = jax.nn.sigmoid(a0 + lo1[:, width:])
    r = shifted(zr_ref, hr_ref, mu_ref[:, :width])
    k = shifted(zk_ref, hk_ref, mu_ref[:, width:2 * width])
    v = shifted(zv_ref, hv_ref, mu_ref[:, 2 * width:3 * width])
    if first:
        v_o[0][0] = v
    else:
        v = v + (vf_ref[0] - v) * jax.nn.sigmoid(v0 + lo2[:, width:])
    vb_o[0] = v.astype(BF16)

    x = k * k_k
    kk = x * lax.rsqrt(jnp.maximum(_segsum(x * x, e_ref[...]), 1e-24))
    k2 = k * (1.0 + (a - 1.0) * k_a)
    b = kk * a
    at_o[0] = (-kk * jnp.exp(cum - lw)).astype(BF16)
    rt_o[0] = (r * jnp.exp(cum)).astype(BF16)
    bt_o[0] = (b * e_inv).astype(BF16)
    kt_o[0] = (k2 * e_inv).astype(BF16)
    bp_o[0] = (b * e_end).astype(BF16)
    kp_o[0] = (k2 * e_end).astype(BF16)
    bv_o[0] = _segsum(r * k2 * r_k, e_ref[...]) * v


def _prep(z, mu, vecs, w1, w2, e, v_first, layer, first, width, d_w, d_g, tt):
    b, t, zw = z.shape
    tt = min(tt, t)
    lw = zw - 4 * width
    hb = tt // SHIFT_HALO
    lcol = 4 * width // lw
    nchunk = tt // WKV_CHUNK
    assert 4 * width % lw == 0 and tt % WKV_CHUNK == 0
    row = jnp.arange(tt)
    same_chunk = (row[:, None] // WKV_CHUNK) == (row[None, :] // WKV_CHUNK)
    tri = jnp.concatenate([same_chunk & (row[None, :] <= row[:, None]), same_chunk], axis=0).astype(BF16)

    def main(col, w):
        return pl.BlockSpec((1, tt, w), lambda bi, i: (bi, i, col))

    def halo(col, w):
        return pl.BlockSpec((1, SHIFT_HALO, w), lambda bi, i: (bi, jnp.maximum(i * hb - 1, 0), col))

    def whole(a):
        return pl.BlockSpec((None,) + a.shape[1:], lambda bi, i: (layer,) + (0,) * (a.ndim - 1))

    tile = pl.BlockSpec((1, tt, width), lambda bi, i: (bi, i, 0))
    full = lambda dt: jax.ShapeDtypeStruct((b, t, width), dt)
    out_specs = [tile] * 7 + [pl.BlockSpec((1, nchunk, 1, width), lambda bi, i: (bi, i, 0, 0)), tile, tile]
    out_shape = [full(BF16)] * 7 + [jax.ShapeDtypeStruct((b, t // WKV_CHUNK, 1, width), F32),
                                    full(F32), full(F32)]
    if first:
        out_specs.append(tile)
        out_shape.append(full(F32))
    return pl.pallas_call(
        functools.partial(_prep_kernel, first, (width, d_w, d_g)),
        grid=(b, t // tt),
        in_specs=[
            main(1, width), main(2, width), main(3, width), main(lcol, lw),
            halo(1, width), halo(2, width), halo(3, width), halo(lcol, lw),
            whole(mu), whole(vecs), whole(w1), whole(w2),
            pl.BlockSpec(e.shape, lambda bi, i: (0, 0)),
            pl.BlockSpec(tri.shape, lambda bi, i: (0, 0)),
            tile,
        ],
        out_specs=out_specs,
        out_shape=out_shape,
        compiler_params=_params("parallel", "parallel"),
        name="rwkv_prep",
    )(z, z, z, z, z, z, z, z, mu, vecs, w1, w2, e, tri, v_first)


def _wkv_kernel(at_ref, rt_ref, bt_ref, kt_ref, bp_ref, kp_ref, v_ref, wl_ref, y_ref, s_ref):
    hg = WKV_HEADS_PER_GROUP
    c = pl.program_id(1)
    nrow, L, C = at_ref.shape
    W, GL = hg * HEAD_SIZE, hg * L
    ng = C // W

    @pl.when(c == 0)
    def _():
        s_ref[...] = jnp.zeros_like(s_ref)

    lane = lax.broadcasted_iota(jnp.int32, (L, W), 1)
    head_mask = [(lane >= h * HEAD_SIZE) & (lane < (h + 1) * HEAD_SIZE) for h in range(hg)]
    row2 = lax.broadcasted_iota(jnp.int32, (L, 2 * GL), 0)
    col2 = lax.broadcasted_iota(jnp.int32, (L, 2 * GL), 1) & (L - 1)
    strict = col2 < row2
    incl = col2 <= row2
    colg = lax.broadcasted_iota(jnp.int32, (L, GL), 1)
    rowg = lax.broadcasted_iota(jnp.int32, (L, GL), 0)
    eye = (colg & (L - 1)) == rowg
    blk_mask = [(colg >= h * L) & (colg < (h + 1) * L) for h in range(hg)]
    sr = lax.broadcasted_iota(jnp.int32, (W, W), 0)
    sc = lax.broadcasted_iota(jnp.int32, (W, W), 1)
    same_head = (sr // HEAD_SIZE) == (sc // HEAD_SIZE)

    def stack_heads(x):
        return jnp.concatenate([jnp.where(m, x, 0.0).astype(BF16) for m in head_mask], axis=0)

    def block_diag(q):
        return jnp.concatenate([jnp.where(m, q, 0.0).astype(BF16) for m in blk_mask], axis=0)

    chains = [(b, g) for b in range(nrow) for g in range(ng)]
    ids = range(len(chains))

    def op(ref, i):
        b, g = chains[i]
        return ref[b, :, g * W:(g + 1) * W]

    lhs = [jnp.concatenate([op(at_ref, i), op(rt_ref, i)], axis=0) for i in ids]
    gm = [_dot_nt(lhs[i], jnp.concatenate([stack_heads(op(bt_ref, i)), stack_heads(op(kt_ref, i))],
                                          axis=0)) for i in ids]
    s0 = [s_ref[i] for i in ids]
    as0 = [_dot_nt(lhs[i], s0[i].astype(BF16)) for i in ids]
    pa = [jnp.where(strict, gm[i][:L], 0.0) for i in ids]
    pr = [jnp.where(incl, gm[i][L:], 0.0) for i in ids]
    vm = [stack_heads(op(v_ref, i)) for i in ids]

    tm = [jnp.where(eye, 1.0, pa[i][:, :GL]) for i in ids]
    q = [_dot(pa[i][:, :GL].astype(BF16), block_diag(pa[i][:, :GL])) for i in ids]
    x1 = [as0[i][:L] + _dot(pa[i][:, GL:].astype(BF16), vm[i]) for i in ids]
    levels = L.bit_length() - 2
    for j in range(levels):
        if j + 1 < levels:
            res = [_dot(jnp.concatenate([q[i].astype(BF16), tm[i].astype(BF16)], axis=0),
                        block_diag(q[i])) for i in ids]
            q = [res[i][:L] for i in ids]
            tm = [tm[i] + res[i][L:] for i in ids]
        else:
            tm = [tm[i] + _dot(tm[i].astype(BF16), block_diag(q[i])) for i in ids]

    u = [_dot(tm[i].astype(BF16), stack_heads(x1[i])) for i in ids]
    for i, (b, g) in enumerate(chains):
        uvm = jnp.concatenate([stack_heads(u[i]), vm[i]], axis=0)
        y_ref[b, :, g * W:(g + 1) * W] = as0[i][L:] + _dot(pr[i].astype(BF16), uvm)
    for i, (b, g) in enumerate(chains):
        uv = jnp.concatenate([u[i].astype(BF16), op(v_ref, i)], axis=0)
        bk = jnp.concatenate([op(bp_ref, i), op(kp_ref, i)], axis=0)
        ds = _dot_tn(uv, bk)
        s_ref[i] = s0[i] * wl_ref[b, 0, :, g * W:(g + 1) * W] + jnp.where(same_head, ds, 0.0)


def _wkv(at, rt, bt, kt, bp, kp, vb, wl):
    bsz, t, c = at.shape
    L = WKV_CHUNK
    w = WKV_HEADS_PER_GROUP * HEAD_SIZE
    assert L & (L - 1) == 0 and t % L == 0 and c % w == 0
    nrow = WKV_ROWS_PER_STEP if bsz % WKV_ROWS_PER_STEP == 0 else 1
    spec = pl.BlockSpec((nrow, L, c), lambda bi, ci: (bi, ci, 0))
    return pl.pallas_call(
        _wkv_kernel,
        grid=(bsz // nrow, t // L),
        in_specs=[spec] * 7 + [pl.BlockSpec((nrow, 1, 1, c), lambda bi, ci: (bi, ci, 0, 0))],
        out_specs=spec,
        out_shape=jax.ShapeDtypeStruct((bsz, t, c), F32),
        scratch_shapes=[pltpu.VMEM((nrow * (c // w), w, w), F32)],
        compiler_params=_params("parallel", "arbitrary"),
        name="wkv_scan",
    )(at, rt, bt, kt, bp, kp, vb, wl)


def _post_kernel(y_ref, bv_ref, g_ref, vec_ref, e_ref, o_ref):
    e = e_ref[...]
    gn_g, gn_b = vec_ref[0:1, :], vec_ref[1:2, :]
    inv_n = 1.0 / HEAD_SIZE
    y = y_ref[0]
    d = y - _segsum(y, e) * inv_n
    var = _segsum(d * d, e) * inv_n
    yn = d * lax.rsqrt(var + GN_EPS) * gn_g + gn_b
    o_ref[0] = ((yn + bv_ref[0]) * g_ref[0]).astype(o_ref.dtype)


def _post(y, bv, g, vecs, e, layer, tt):
    b, t, c = y.shape
    tt = min(tt, t)
    spec = pl.BlockSpec((1, tt, c), lambda bi, i: (bi, i, 0))
    return pl.pallas_call(
        _post_kernel,
        grid=(b, t // tt),
        in_specs=[spec] * 3 + [
            pl.BlockSpec((None,) + vecs.shape[1:], lambda bi, i: (layer, 0, 0)),
            pl.BlockSpec(e.shape, lambda bi, i: (0, 0)),
        ],
        out_specs=spec,
        out_shape=jax.ShapeDtypeStruct((b, t, c), BF16),
        compiler_params=_params("parallel", "parallel"),
        name="rwkv_post",
    )(y, bv, g, vecs, e)


def kernel(x, p, attn_norm, w_in, mu_shift, w_vres_dn, mu_vres, v0, v_up, pool_w, pool_scale,
           w0, w_up, a0, a_up, g_up, k_k, k_a, r_k, gn_g, gn_b, w_out, mlp_norm, w_ffn_up,
           w_ffn_down, ple_norm, w_ple_gate, w_ple_proj, final_norm):
    bsz, t, d = x.shape
    depth = w_in.shape[0]
    width = w0.shape[1]
    pool_width = pool_scale.shape[1]
    d_w, d_a, d_g, d_v = w_up.shape[1], a_up.shape[1], g_up.shape[1], v_up.shape[1]
    d_ple = p.shape[-1]
    m = bsz * t
    assert pool_width == width and d_w + d_a == LANE and width % SEG_SLAB == 0
    lora_w = 2 * SEG_SLAB
    w2_rows = lora_w - LANE - LANE
    assert d_g + d_v <= w2_rows
    zw = pool_width + 3 * width + lora_w

    n_in = w_in.shape[2]
    vres = jnp.concatenate([jnp.zeros((1, d, d_v), F32), w_vres_dn], axis=0)
    w_in_x = jnp.concatenate(
        [w_in, vres, jnp.zeros((depth, d, zw - n_in - d_v), F32)], axis=2).astype(BF16)
    mu_vres_x = jnp.concatenate([jnp.zeros((1, d_v), F32), mu_vres], axis=0)
    mu_x = jnp.concatenate(
        [mu_shift, mu_vres_x, jnp.zeros((depth, zw - n_in - d_v), F32)], axis=1)[:, None, :]
    zeros_w = lambda rows: jnp.zeros((depth, rows, width), F32)
    w1 = jnp.concatenate([
        jnp.concatenate([w_up, zeros_w(d_w)], axis=2),
        jnp.concatenate([zeros_w(d_a), a_up], axis=2)], axis=1).astype(BF16)
    v_up_x = jnp.concatenate([jnp.zeros((1, d_v, width), F32), v_up], axis=0)
    w2 = jnp.concatenate([
        jnp.concatenate([g_up, zeros_w(d_g)], axis=2),
        jnp.concatenate([zeros_w(d_v), v_up_x], axis=2),
        jnp.zeros((depth, w2_rows - d_g - d_v, 2 * width), F32)], axis=1).astype(BF16)
    v0_x = jnp.concatenate([jnp.zeros((1, width), F32), v0], axis=0)
    pad = lambda rows: jnp.zeros((depth, rows, width), F32)
    prep_vecs = jnp.concatenate(
        [jnp.stack([w0, a0, v0_x, k_k, k_a, r_k.reshape(depth, width)], axis=1), pad(2)],
        axis=1)
    post_vecs = jnp.concatenate([jnp.stack([gn_g, gn_b], axis=1), pad(6)], axis=1)
    seg = jnp.arange(SEG_SLAB) // HEAD_SIZE
    e = (seg[:, None] == seg[None, :]).astype(BF16)
    attn_g, mlp_g, ple_g = attn_norm[:, None, :], mlp_norm[:, None, :], ple_norm[:, None, :]
    pool_wb = pool_w.astype(BF16)
    pool_sc = pool_scale[:, None, :]
    w_out_b = w_out.astype(BF16)
    w_up_b, w_down_b = w_ffn_up.astype(BF16), w_ffn_down.astype(BF16)
    w_gate_b, w_proj_b = w_ple_gate.astype(BF16), w_ple_proj.astype(BF16)
    p2 = p.reshape(depth * m, d_ple)
    final_g = final_norm[None, :]

    xf = x.reshape(m, d)
    v_first = None
    for i in range(depth):
        z = _rms_matmul(xf, attn_g, w_in_x, i, tm=512, tn=768).reshape(bsz, t, zw)
        pool_out = _pool(z, pool_wb, pool_sc, i, pool_width, tt=512)
        first = i == 0
        at, rt, bt, kt, bp, kp, vb, wl, bv, g, *v_new = _prep(
            z, mu_x, prep_vecs, w1, w2, e, z if first else v_first, i, first, width, d_w, d_g, tt=256)
        if first:
            v_first = v_new[0]
        y = _wkv(at, rt, bt, kt, bp, kp, vb, wl)
        rwkv_out = _post(y, bv, g, post_vecs, e, i, tt=512)
        xf = _outproj(xf, pool_out.reshape(m, pool_width), rwkv_out.reshape(m, width), w_out_b, i,
                      tm=512, tn=512)
        xf = _ffn(xf, mlp_g, w_up_b, w_down_b, i, tm=512, tf=1024)
        xf = _ple(xf, ple_g, w_gate_b, p2, w_proj_b, final_g, i, i == depth - 1, tm=512)
    return xf.reshape(bsz, t, d)
```

```python
import functools

import jax
import jax.numpy as jnp
from jax import lax
from jax.experimental import pallas as pl
from jax.experimental.pallas import tpu as pltpu

F32 = jnp.float32
BF16 = jnp.bfloat16

NORM_EPS = 1e-6
GN_EPS = 64e-5
EXP_NEG_HALF = 0.6065306597126334
HEAD_SIZE = 64
POOL_WINDOWS = (2, 4, 8, 16)
POOL_HALO = 16
SHIFT_HALO = 8
WKV_CHUNK = 64
WKV_HEADS_PER_GROUP = 2
WKV_ROWS_PER_STEP = 4
SEG_SLAB = 256
LANE = 128
VMEM_LIMIT = 52 * 1024 * 1024


def _dot(a, b):
    return jnp.dot(a, b, preferred_element_type=F32)


def _dot_nt(a, b):
    return lax.dot_general(a, b, (((1,), (1,)), ((), ())), preferred_element_type=F32)


def _dot_tn(a, b):
    return lax.dot_general(a, b, (((0,), (0,)), ((), ())), preferred_element_type=F32)


def _params(*sem):
    return pltpu.CompilerParams(dimension_semantics=sem, vmem_limit_bytes=VMEM_LIMIT)


def _rms_rows(x, g):
    ms = jnp.mean(x * x, axis=-1, keepdims=True)
    return x * lax.rsqrt(ms + NORM_EPS) * g


def _norm_into(h_ref, x_ref, g_ref, rows=256):
    rows = min(rows, x_ref.shape[0])
    n = x_ref.shape[0] // rows

    def body(c, carry):
        rs = pl.ds(pl.multiple_of(c * rows, rows), rows)
        h_ref[rs, :] = _rms_rows(x_ref[rs, :], g_ref[...]).astype(h_ref.dtype)
        return carry

    lax.fori_loop(0, n, body, 0)


def _segsum(x, e):
    outs = []
    for s in range(x.shape[1] // SEG_SLAB):
        xs = x[:, s * SEG_SLAB:(s + 1) * SEG_SLAB]
        hi = xs.astype(BF16)
        lo = (xs - hi.astype(F32)).astype(BF16)
        outs.append(_dot(hi, e) + _dot(lo, e))
    return jnp.concatenate(outs, axis=1)


def _row_parts(n_rows, parts=2):
    step = n_rows // parts
    return [slice(s * step, (s + 1) * step) for s in range(parts)]


def _rms_matmul_kernel(tn, x_ref, g_ref, w_ref, o_ref):
    g = g_ref[...]
    parts = _row_parts(x_ref.shape[0])
    hs = [_rms_rows(x_ref[rs, :], g).astype(BF16) for rs in parts]
    for rs, h in zip(parts, hs):
        for c in range(w_ref.shape[1] // tn):
            cs = slice(c * tn, (c + 1) * tn)
            o_ref[rs, cs] = _dot(h, w_ref[:, cs])


def _resident(block_shape, index_map):
    return pl.BlockSpec(block_shape, index_map, pipeline_mode=pl.Buffered(1))


def _rms_matmul(x, g, w, layer, tm, tn):
    m, d = x.shape
    n = w.shape[2]
    tm, tn = min(tm, m), min(tn, n)
    return pl.pallas_call(
        functools.partial(_rms_matmul_kernel, tn),
        grid=(m // tm,),
        in_specs=[
            pl.BlockSpec((tm, d), lambda i: (i, 0)),
            pl.BlockSpec((None, 1, d), lambda i: (layer, 0, 0)),
            _resident((None, d, n), lambda i: (layer, 0, 0)),
        ],
        out_specs=pl.BlockSpec((tm, n), lambda i: (i, 0)),
        out_shape=jax.ShapeDtypeStruct((m, n), F32),
        compiler_params=_params("parallel"),
        name="rms_matmul",
    )(x, g, w)


def _outproj_kernel(tn, x_ref, ma_ref, mb_ref, w_ref, o_ref):
    ka = ma_ref.shape[1]
    for c in range(x_ref.shape[1] // tn):
        cs = slice(c * tn, (c + 1) * tn)
        o_ref[:, cs] = (x_ref[:, cs] + _dot(ma_ref[...], w_ref[:ka, cs])
                        + _dot(mb_ref[...], w_ref[ka:, cs]))


def _outproj(x, mix_a, mix_b, w, layer, tm, tn):
    m, d = x.shape
    ka, kb = mix_a.shape[1], mix_b.shape[1]
    tm, tn = min(tm, m), min(tn, d)
    return pl.pallas_call(
        functools.partial(_outproj_kernel, tn),
        grid=(m // tm,),
        in_specs=[
            pl.BlockSpec((tm, d), lambda i: (i, 0)),
            pl.BlockSpec((tm, ka), lambda i: (i, 0)),
            pl.BlockSpec((tm, kb), lambda i: (i, 0)),
            _resident((None, ka + kb, d), lambda i: (layer, 0, 0)),
        ],
        out_specs=pl.BlockSpec((tm, d), lambda i: (i, 0)),
        out_shape=jax.ShapeDtypeStruct((m, d), F32),
        compiler_params=_params("parallel"),
        name="outproj",
    )(x, mix_a, mix_b, w)


def _ffn_kernel(x_ref, g_ref, wu_ref, wd_ref, o_ref, h_ref):
    @pl.when(pl.program_id(1) == 0)
    def _():
        _norm_into(h_ref, x_ref, g_ref)
        o_ref[...] = x_ref[...]

    a = _dot(h_ref[...], wu_ref[...])
    a = jnp.square(jnp.maximum(a, 0.0)).astype(BF16)
    o_ref[...] += _dot(a, wd_ref[...])


def _ffn(x, g, w_up, w_down, layer, tm, tf):
    m, d = x.shape
    f = w_up.shape[2]
    tm, tf = min(tm, m), min(tf, f)
    return pl.pallas_call(
        _ffn_kernel,
        grid=(m // tm, f // tf),
        in_specs=[
            pl.BlockSpec((tm, d), lambda i, j: (i, 0)),
            pl.BlockSpec((None, 1, d), lambda i, j: (layer, 0, 0)),
            pl.BlockSpec((None, d, tf), lambda i, j: (layer, 0, j)),
            pl.BlockSpec((None, tf, d), lambda i, j: (layer, j, 0)),
        ],
        out_specs=pl.BlockSpec((tm, d), lambda i, j: (i, 0)),
        out_shape=jax.ShapeDtypeStruct((m, d), F32),
        scratch_shapes=[pltpu.VMEM((tm, d), BF16)],
        compiler_params=_params("parallel", "arbitrary"),
        name="ffn",
    )(x, g, w_up, w_down)


def _ple_kernel(final, tn, x_ref, g_ref, wg_ref, p_ref, wp_ref, fg_ref, o_ref):
    g = g_ref[...]
    parts = _row_parts(x_ref.shape[0])
    hs = [_rms_rows(x_ref[rs, :], g).astype(BF16) for rs in parts]
    for rs, h in zip(parts, hs):
        pb = p_ref[rs, :].astype(BF16)
        for c in range(x_ref.shape[1] // tn):
            cs = slice(c * tn, (c + 1) * tn)
            gate = jax.nn.sigmoid(_dot(h, wg_ref[:, cs]))
            o_ref[rs, cs] = x_ref[rs, cs] + gate * _dot(pb, wp_ref[:, cs])
    if final:
        _norm_into(o_ref, o_ref, fg_ref)


def _ple(x, g, w_gate, p, w_proj, final_g, layer, final, tm):
    m, d = x.shape
    dp = p.shape[1]
    tm = min(tm, m)
    nblk = m // tm
    return pl.pallas_call(
        functools.partial(_ple_kernel, final, min(512, d)),
        grid=(nblk,),
        in_specs=[
            pl.BlockSpec((tm, d), lambda i: (i, 0)),
            pl.BlockSpec((None, 1, d), lambda i: (layer, 0, 0)),
            _resident((None, d, d), lambda i: (layer, 0, 0)),
            pl.BlockSpec((tm, dp), lambda i: (layer * nblk + i, 0)),
            _resident((None, dp, d), lambda i: (layer, 0, 0)),
            pl.BlockSpec((1, d), lambda i: (0, 0)),
        ],
        out_specs=pl.BlockSpec((tm, d), lambda i: (i, 0)),
        out_shape=jax.ShapeDtypeStruct((m, d), F32),
        compiler_params=_params("parallel"),
        name="ple",
    )(x, g, w_gate, p, w_proj, final_g)


def _pool_kernel(z_ref, halo_ref, pw_ref, ps_ref, o_ref):
    i = pl.program_id(1)
    tt, width = z_ref.shape[1], z_ref.shape[2]
    pg = width // len(POOL_WINDOWS)
    u = z_ref[0]
    halo = jnp.where(i == 0, 0.0, halo_ref[0])
    cur = jnp.concatenate([halo, u], axis=0)
    span, lane0 = 1, 0
    t = i * tt + lax.broadcasted_iota(jnp.int32, (tt, pg), 0)
    for gi, win in enumerate(POOL_WINDOWS):
        cur = cur[:, gi * pg - lane0:]
        lane0 = gi * pg
        while span < win:
            cur = cur + pltpu.roll(cur, span, 0)
            span *= 2
        cols = slice(gi * pg, (gi + 1) * pg)
        cnt = jnp.minimum(t + 1, win).astype(F32)
        d = cur[POOL_HALO:, :pg] / cnt - u[:, cols]
        o = _dot(d.astype(BF16), pw_ref[gi]) * ps_ref[:, cols]
        o_ref[0, :, cols] = o.astype(o_ref.dtype)


def _pool(z, pool_w, pool_scale, layer, width, tt):
    b, t, _ = z.shape
    g, pg, _ = pool_w.shape[1:]
    tt = min(tt, t)
    assert all(w & (w - 1) == 0 for w in POOL_WINDOWS) and max(POOL_WINDOWS) <= POOL_HALO
    assert list(POOL_WINDOWS) == sorted(POOL_WINDOWS) and g == len(POOL_WINDOWS) and g * pg == width
    hb = tt // POOL_HALO
    return pl.pallas_call(
        _pool_kernel,
        grid=(b, t // tt),
        in_specs=[
            pl.BlockSpec((1, tt, width), lambda bi, i: (bi, i, 0)),
            pl.BlockSpec((1, POOL_HALO, width), lambda bi, i: (bi, jnp.maximum(i * hb - 1, 0), 0)),
            pl.BlockSpec((None, g, pg, pg), lambda bi, i: (layer, 0, 0, 0)),
            pl.BlockSpec((None, 1, width), lambda bi, i: (layer, 0, 0)),
        ],
        out_specs=pl.BlockSpec((1, tt, width), lambda bi, i: (bi, i, 0)),
        out_shape=jax.ShapeDtypeStruct((b, t, width), BF16),
        compiler_params=_params("parallel", "parallel"),
        name="pool",
    )(z, z, pool_w, pool_scale)


def _prep_kernel(first, dims, zr_ref, zk_ref, zv_ref, zl_ref, hr_ref, hk_ref, hv_ref, hl_ref,
                 mu_ref, vec_ref, w1_ref, w2_ref, e_ref, tri_ref, vf_ref,
                 at_o, rt_o, bt_o, kt_o, bp_o, kp_o, vb_o, wl_o, bv_o, g_o, *v_o):
    width, d_w, d_g = dims
    i = pl.program_id(1)
    tt = zr_ref.shape[1]

    def shifted(z_ref, h_ref, mu):
        z = z_ref[0]
        prev = jnp.where(i == 0, 0.0, h_ref[0, SHIFT_HALO - 1:SHIFT_HALO, :])
        zp = pltpu.roll(z, 1, 0)
        row = lax.broadcasted_iota(jnp.int32, z.shape, 0)
        zp = jnp.where(row == 0, prev, zp)
        return z + (zp - z) * mu

    w0, a0, v0 = vec_ref[0:1, :], vec_ref[1:2, :], vec_ref[2:3, :]
    k_k, k_a, r_k = vec_ref[3:4, :], vec_ref[4:5, :], vec_ref[5:6, :]

    lz = shifted(zl_ref, hl_ref, mu_ref[:, 3 * width:])
    s1 = lz[:, :LANE]
    lane1 = lax.broadcasted_iota(jnp.int32, s1.shape, 1)
    act1 = jnp.where(lane1 < d_w, jnp.tanh(s1), s1)
    lo1 = _dot(act1.astype(BF16), w1_ref[...])
    s2 = lz[:, LANE:LANE + w2_ref.shape[0]]
    lane2 = lax.broadcasted_iota(jnp.int32, s2.shape, 1)
    act2 = jnp.where(lane2 < d_g, jax.nn.sigmoid(s2), s2)
    lo2 = _dot(act2.astype(BF16), w2_ref[...])
    g_o[0] = lo2[:, :width]

    lw = -EXP_NEG_HALF * jax.nn.sigmoid(w0 + lo1[:, :width])
    h1 = lw.astype(BF16)
    r1 = lw - h1.astype(F32)
    h2 = r1.astype(BF16)
    h3 = (r1 - h2.astype(F32)).astype(BF16)
    tri = tri_ref[...]
    sums = _dot(tri, h1) + _dot(tri, h2) + _dot(tri, h3)
    cum, tot = sums[:tt], sums[tt:]
    e_inv = jnp.exp(-cum)
    e_end = jnp.exp(tot - cum)
    for c in range(tt // WKV_CHUNK):
        wl_o[0, c] = jnp.exp(tot[c * WKV_CHUNK:c * WKV_CHUNK + 1, :])

    a = jax.nn.sigmoid(a0 + lo1[:, width:])
    r = shifted(zr_ref, hr_ref, mu_ref[:, :width])
    k = shifted(zk_ref, hk_ref, mu_ref[:, width:2 * width])
    v = shifted(zv_ref, hv_ref, mu_ref[:, 2 * width:3 * width])
    if first:
        v_o[0][0] = v
    else:
        v = v + (vf_ref[0] - v) * jax.nn.sigmoid(v0 + lo2[:, width:])
    vb_o[0] = v.astype(BF16)

    x = k * k_k
    kk = x * lax.rsqrt(jnp.maximum(_segsum(x * x, e_ref[...]), 1e-24))
    k2 = k * (1.0 + (a - 1.0) * k_a)
    b = kk * a
    at_o[0] = (-kk * jnp.exp(cum - lw)).astype(BF16)
    rt_o[0] = (r * jnp.exp(cum)).astype(BF16)
    bt_o[0] = (b * e_inv).astype(BF16)
    kt_o[0] = (k2 * e_inv).astype(BF16)
    bp_o[0] = (b * e_end).astype(BF16)
    kp_o[0] = (k2 * e_end).astype(BF16)
    bv_o[0] = _segsum(r * k2 * r_k, e_ref[...]) * v


def _prep(z, mu, vecs, w1, w2, e, v_first, layer, first, width, d_w, d_g, tt):
    b, t, zw = z.shape
    tt = min(tt, t)
    lw = zw - 4 * width
    hb = tt // SHIFT_HALO
    lcol = 4 * width // lw
    nchunk = tt // WKV_CHUNK
    assert 4 * width % lw == 0 and tt % WKV_CHUNK == 0
    row = jnp.arange(tt)
    same_chunk = (row[:, None] // WKV_CHUNK) == (row[None, :] // WKV_CHUNK)
    tri = jnp.concatenate([same_chunk & (row[None, :] <= row[:, None]), same_chunk], axis=0).astype(BF16)

    def main(col, w):
        return pl.BlockSpec((1, tt, w), lambda bi, i: (bi, i, col))

    def halo(col, w):
        return pl.BlockSpec((1, SHIFT_HALO, w), lambda bi, i: (bi, jnp.maximum(i * hb - 1, 0), col))

    def whole(a):
        return pl.BlockSpec((None,) + a.shape[1:], lambda bi, i: (layer,) + (0,) * (a.ndim - 1))

    tile = pl.BlockSpec((1, tt, width), lambda bi, i: (bi, i, 0))
    full = lambda dt: jax.ShapeDtypeStruct((b, t, width), dt)
    out_specs = [tile] * 7 + [pl.BlockSpec((1, nchunk, 1, width), lambda bi, i: (bi, i, 0, 0)), tile, tile]
    out_shape = [full(BF16)] * 7 + [jax.ShapeDtypeStruct((b, t // WKV_CHUNK, 1, width), F32),
                                    full(F32), full(F32)]
    if first:
        out_specs.append(tile)
        out_shape.append(full(F32))
    return pl.pallas_call(
        functools.partial(_prep_kernel, first, (width, d_w, d_g)),
        grid=(b, t // tt),
        in_specs=[
            main(1, width), main(2, width), main(3, width), main(lcol, lw),
            halo(1, width), halo(2, width), halo(3, width), halo(lcol, lw),
            whole(mu), whole(vecs), whole(w1), whole(w2),
            pl.BlockSpec(e.shape, lambda bi, i: (0, 0)),
            pl.BlockSpec(tri.shape, lambda bi, i: (0, 0)),
            tile,
        ],
        out_specs=out_specs,
        out_shape=out_shape,
        compiler_params=_params("parallel", "parallel"),
        name="rwkv_prep",
    )(z, z, z, z, z, z, z, z, mu, vecs, w1, w2, e, tri, v_first)


def _wkv_kernel(at_ref, rt_ref, bt_ref, kt_ref, bp_ref, kp_ref, v_ref, wl_ref, y_ref, s_ref):
    hg = WKV_HEADS_PER_GROUP
    c = pl.program_id(1)
    nrow, L, C = at_ref.shape
    W, GL = hg * HEAD_SIZE, hg * L
    ng = C // W

    @pl.when(c == 0)
    def _():
        s_ref[...] = jnp.zeros_like(s_ref)

    lane = lax.broadcasted_iota(jnp.int32, (L, W), 1)
    head_mask = [(lane >= h * HEAD_SIZE) & (lane < (h + 1) * HEAD_SIZE) for h in range(hg)]
    row2 = lax.broadcasted_iota(jnp.int32, (L, 2 * GL), 0)
    col2 = lax.broadcasted_iota(jnp.int32, (L, 2 * GL), 1) & (L - 1)
    strict = col2 < row2
    incl = col2 <= row2
    colg = lax.broadcasted_iota(jnp.int32, (L, GL), 1)
    rowg = lax.broadcasted_iota(jnp.int32, (L, GL), 0)
    eye = (colg & (L - 1)) == rowg
    blk_mask = [(colg >= h * L) & (colg < (h + 1) * L) for h in range(hg)]
    sr = lax.broadcasted_iota(jnp.int32, (W, W), 0)
    sc = lax.broadcasted_iota(jnp.int32, (W, W), 1)
    same_head = (sr // HEAD_SIZE) == (sc // HEAD_SIZE)

    def stack_heads(x):
        return jnp.concatenate([jnp.where(m, x, 0.0).astype(BF16) for m in head_mask], axis=0)

    def block_diag(q):
        return jnp.concatenate([jnp.where(m, q, 0.0).astype(BF16) for m in blk_mask], axis=0)

    chains = [(b, g) for b in range(nrow) for g in range(ng)]
    ids = range(len(chains))

    def op(ref, i):
        b, g = chains[i]
        return ref[b, :, g * W:(g + 1) * W]

    lhs = [jnp.concatenate([op(at_ref, i), op(rt_ref, i)], axis=0) for i in ids]
    gm = [_dot_nt(lhs[i], jnp.concatenate([stack_heads(op(bt_ref, i)), stack_heads(op(kt_ref, i))],
                                          axis=0)) for i in ids]
    s0 = [s_ref[i] for i in ids]
    as0 = [_dot_nt(lhs[i], s0[i].astype(BF16)) for i in ids]
    pa = [jnp.where(strict, gm[i][:L], 0.0) for i in ids]
    pr = [jnp.where(incl, gm[i][L:], 0.0) for i in ids]
    vm = [stack_heads(op(v_ref, i)) for i in ids]

    tm = [jnp.where(eye, 1.0, pa[i][:, :GL]) for i in ids]
    q = [_dot(pa[i][:, :GL].astype(BF16), block_diag(pa[i][:, :GL])) for i in ids]
    pkv = [_dot(jnp.concatenate([pa[i][:, GL:].astype(BF16), pr[i][:, GL:].astype(BF16)], axis=0),
                vm[i]) for i in ids]
    x1 = [as0[i][:L] + pkv[i][:L] for i in ids]
    levels = L.bit_length() - 2
    for j in range(levels):
        if j + 1 < levels:
            res = [_dot(jnp.concatenate([q[i].astype(BF16), tm[i].astype(BF16)], axis=0),
                        block_diag(q[i])) for i in ids]
            q = [res[i][:L] for i in ids]
            tm = [tm[i] + res[i][L:] for i in ids]
        else:
            tm = [tm[i] + _dot(tm[i].astype(BF16), block_diag(q[i])) for i in ids]

    u = [_dot(tm[i].astype(BF16), stack_heads(x1[i])) for i in ids]
    for i, (b, g) in enumerate(chains):
        y_ref[b, :, g * W:(g + 1) * W] = (as0[i][L:] + pkv[i][L:]
                                          + _dot(pr[i][:, :GL].astype(BF16), stack_heads(u[i])))
    for i, (b, g) in enumerate(chains):
        uv = jnp.concatenate([u[i].astype(BF16), op(v_ref, i)], axis=0)
        bk = jnp.concatenate([op(bp_ref, i), op(kp_ref, i)], axis=0)
        ds = _dot_tn(uv, bk)
        s_ref[i] = s0[i] * wl_ref[b, 0, :, g * W:(g + 1) * W] + jnp.where(same_head, ds, 0.0)


def _wkv(at, rt, bt, kt, bp, kp, vb, wl):
    bsz, t, c = at.shape
    L = WKV_CHUNK
    w = WKV_HEADS_PER_GROUP * HEAD_SIZE
    assert L & (L - 1) == 0 and t % L == 0 and c % w == 0
    nrow = WKV_ROWS_PER_STEP if bsz % WKV_ROWS_PER_STEP == 0 else 1
    spec = pl.BlockSpec((nrow, L, c), lambda bi, ci: (bi, ci, 0))
    return pl.pallas_call(
        _wkv_kernel,
        grid=(bsz // nrow, t // L),
        in_specs=[spec] * 7 + [pl.BlockSpec((nrow, 1, 1, c), lambda bi, ci: (bi, ci, 0, 0))],
        out_specs=spec,
        out_shape=jax.ShapeDtypeStruct((bsz, t, c), F32),
        scratch_shapes=[pltpu.VMEM((nrow * (c // w), w, w), F32)],
        compiler_params=_params("parallel", "arbitrary"),
        name="wkv_scan",
    )(at, rt, bt, kt, bp, kp, vb, wl)


def _post_kernel(y_ref, bv_ref, g_ref, vec_ref, e_ref, o_ref):
    e = e_ref[...]
    gn_g, gn_b = vec_ref[0:1, :], vec_ref[1:2, :]
    inv_n = 1.0 / HEAD_SIZE
    y = y_ref[0]
    d = y - _segsum(y, e) * inv_n
    var = _segsum(d * d, e) * inv_n
    yn = d * lax.rsqrt(var + GN_EPS) * gn_g + gn_b
    o_ref[0] = ((yn + bv_ref[0]) * g_ref[0]).astype(o_ref.dtype)


def _post(y, bv, g, vecs, e, layer, tt):
    b, t, c = y.shape
    tt = min(tt, t)
    spec = pl.BlockSpec((1, tt, c), lambda bi, i: (bi, i, 0))
    return pl.pallas_call(
        _post_kernel,
        grid=(b, t // tt),
        in_specs=[spec] * 3 + [
            pl.BlockSpec((None,) + vecs.shape[1:], lambda bi, i: (layer, 0, 0)),
            pl.BlockSpec(e.shape, lambda bi, i: (0, 0)),
        ],
        out_specs=spec,
        out_shape=jax.ShapeDtypeStruct((b, t, c), BF16),
        compiler_params=_params("parallel", "parallel"),
        name="rwkv_post",
    )(y, bv, g, vecs, e)


def kernel(x, p, attn_norm, w_in, mu_shift, w_vres_dn, mu_vres, v0, v_up, pool_w, pool_scale,
           w0, w_up, a0, a_up, g_up, k_k, k_a, r_k, gn_g, gn_b, w_out, mlp_norm, w_ffn_up,
           w_ffn_down, ple_norm, w_ple_gate, w_ple_proj, final_norm):
    bsz, t, d = x.shape
    depth = w_in.shape[0]
    width = w0.shape[1]
    pool_width = pool_scale.shape[1]
    d_w, d_a, d_g, d_v = w_up.shape[1], a_up.shape[1], g_up.shape[1], v_up.shape[1]
    d_ple = p.shape[-1]
    m = bsz * t
    assert pool_width == width and d_w + d_a == LANE and width % SEG_SLAB == 0
    lora_w = 2 * SEG_SLAB
    w2_rows = lora_w - LANE - LANE
    assert d_g + d_v <= w2_rows
    zw = pool_width + 3 * width + lora_w

    n_in = w_in.shape[2]
    vres = jnp.concatenate([jnp.zeros((1, d, d_v), F32), w_vres_dn], axis=0)
    w_in_x = jnp.concatenate(
        [w_in.astype(BF16), vres.astype(BF16), jnp.zeros((depth, d, zw - n_in - d_v), BF16)], axis=2)
    mu_vres_x = jnp.concatenate([jnp.zeros((1, d_v), F32), mu_vres], axis=0)
    mu_x = jnp.concatenate(
        [mu_shift, mu_vres_x, jnp.zeros((depth, zw - n_in - d_v), F32)], axis=1)[:, None, :]
    zeros_w = lambda rows: jnp.zeros((depth, rows, width), F32)
    w1 = jnp.concatenate([
        jnp.concatenate([w_up, zeros_w(d_w)], axis=2),
        jnp.concatenate([zeros_w(d_a), a_up], axis=2)], axis=1).astype(BF16)
    v_up_x = jnp.concatenate([jnp.zeros((1, d_v, width), F32), v_up], axis=0)
    w2 = jnp.concatenate([
        jnp.concatenate([g_up, zeros_w(d_g)], axis=2),
        jnp.concatenate([zeros_w(d_v), v_up_x], axis=2),
        jnp.zeros((depth, w2_rows - d_g - d_v, 2 * width), F32)], axis=1).astype(BF16)
    v0_x = jnp.concatenate([jnp.zeros((1, width), F32), v0], axis=0)
    pad = lambda rows: jnp.zeros((depth, rows, width), F32)
    prep_vecs = jnp.concatenate(
        [jnp.stack([w0, a0, v0_x, k_k, k_a, r_k.reshape(depth, width)], axis=1), pad(2)],
        axis=1)
    post_vecs = jnp.concatenate([jnp.stack([gn_g, gn_b], axis=1), pad(6)], axis=1)
    seg = jnp.arange(SEG_SLAB) // HEAD_SIZE
    e = (seg[:, None] == seg[None, :]).astype(BF16)
    attn_g, mlp_g, ple_g = attn_norm[:, None, :], mlp_norm[:, None, :], ple_norm[:, None, :]
    pool_wb = pool_w.astype(BF16)
    pool_sc = pool_scale[:, None, :]
    w_out_b = w_out.astype(BF16)
    w_up_b, w_down_b = w_ffn_up.astype(BF16), w_ffn_down.astype(BF16)
    w_gate_b, w_proj_b = w_ple_gate.astype(BF16), w_ple_proj.astype(BF16)
    p2 = p.reshape(depth * m, d_ple)
    final_g = final_norm[None, :]

    xf = x.reshape(m, d)
    v_first = None
    for i in range(depth):
        z = _rms_matmul(xf, attn_g, w_in_x, i, tm=512, tn=768).reshape(bsz, t, zw)
        pool_out = _pool(z, pool_wb, pool_sc, i, pool_width, tt=512)
        first = i == 0
        at, rt, bt, kt, bp, kp, vb, wl, bv, g, *v_new = _prep(
            z, mu_x, prep_vecs, w1, w2, e, z if first else v_first, i, first, width, d_w, d_g, tt=256)
        if first:
            v_first = v_new[0]
        y = _wkv(at, rt, bt, kt, bp, kp, vb, wl)
        rwkv_out = _post(y, bv, g, post_vecs, e, i, tt=512)
        xf = _outproj(xf, pool_out.reshape(m, pool_width), rwkv_out.reshape(m, width), w_out_b, i,
                      tm=512, tn=512)
        xf = _ffn(xf, mlp_g, w_up_b, w_down_b, i, tm=512, tf=1024)
        xf = _ple(xf, ple_g, w_gate_b, p2, w_proj_b, final_g, i, i == depth - 1, tm=512)
    return xf.reshape(bsz, t, d)
```

```python
import functools

import jax
import jax.numpy as jnp
from jax import lax
from jax.experimental import pallas as pl
from jax.experimental.pallas import tpu as pltpu

F32 = jnp.float32
BF16 = jnp.bfloat16

NORM_EPS = 1e-6
GN_EPS = 64e-5
EXP_NEG_HALF = 0.6065306597126334
HEAD_SIZE = 64
POOL_WINDOWS = (2, 4, 8, 16)
POOL_HALO = 16
SHIFT_HALO = 8
WKV_CHUNK = 64
WKV_HEADS_PER_GROUP = 2
WKV_ROWS_PER_STEP = 4
SEG_SLAB = 256
LANE = 128
VMEM_LIMIT = 52 * 1024 * 1024


def _dot(a, b):
    return jnp.dot(a, b, preferred_element_type=F32)


def _dot_nt(a, b):
    return lax.dot_general(a, b, (((1,), (1,)), ((), ())), preferred_element_type=F32)


def _dot_tn(a, b):
    return lax.dot_general(a, b, (((0,), (0,)), ((), ())), preferred_element_type=F32)


def _params(*sem):
    return pltpu.CompilerParams(dimension_semantics=sem, vmem_limit_bytes=VMEM_LIMIT)


def _rms_rows(x, g):
    ms = jnp.mean(x * x, axis=-1, keepdims=True)
    return x * lax.rsqrt(ms + NORM_EPS) * g


def _norm_into(h_ref, x_ref, g_ref, rows=256):
    rows = min(rows, x_ref.shape[0])
    n = x_ref.shape[0] // rows

    def body(c, carry):
        rs = pl.ds(pl.multiple_of(c * rows, rows), rows)
        h_ref[rs, :] = _rms_rows(x_ref[rs, :], g_ref[...]).astype(h_ref.dtype)
        return carry

    lax.fori_loop(0, n, body, 0)


def _segsum(x, e):
    outs = []
    for s in range(x.shape[1] // SEG_SLAB):
        xs = x[:, s * SEG_SLAB:(s + 1) * SEG_SLAB]
        hi = xs.astype(BF16)
        lo = (xs - hi.astype(F32)).astype(BF16)
        outs.append(_dot(hi, e) + _dot(lo, e))
    return jnp.concatenate(outs, axis=1)


def _row_parts(n_rows, parts=2):
    step = n_rows // parts
    return [slice(s * step, (s + 1) * step) for s in range(parts)]


def _rms_matmul_kernel(tn, x_ref, g_ref, w_ref, o_ref):
    g = g_ref[...]
    parts = _row_parts(x_ref.shape[0])
    hs = [_rms_rows(x_ref[rs, :], g).astype(BF16) for rs in parts]
    for rs, h in zip(parts, hs):
        for c in range(w_ref.shape[1] // tn):
            cs = slice(c * tn, (c + 1) * tn)
            o_ref[rs, cs] = _dot(h, w_ref[:, cs])


def _resident(block_shape, index_map):
    return pl.BlockSpec(block_shape, index_map, pipeline_mode=pl.Buffered(1))


def _rms_matmul(x, g, w, layer, tm, tn):
    m, d = x.shape
    n = w.shape[2]
    tm, tn = min(tm, m), min(tn, n)
    return pl.pallas_call(
        functools.partial(_rms_matmul_kernel, tn),
        grid=(m // tm,),
        in_specs=[
            pl.BlockSpec((tm, d), lambda i: (i, 0)),
            pl.BlockSpec((None, 1, d), lambda i: (layer, 0, 0)),
            _resident((None, d, n), lambda i: (layer, 0, 0)),
        ],
        out_specs=pl.BlockSpec((tm, n), lambda i: (i, 0)),
        out_shape=jax.ShapeDtypeStruct((m, n), F32),
        compiler_params=_params("parallel"),
        name="rms_matmul",
    )(x, g, w)


def _outproj_kernel(tn, x_ref, ma_ref, mb_ref, w_ref, o_ref):
    ka = ma_ref.shape[1]
    for c in range(x_ref.shape[1] // tn):
        cs = slice(c * tn, (c + 1) * tn)
        o_ref[:, cs] = (x_ref[:, cs] + _dot(ma_ref[...], w_ref[:ka, cs])
                        + _dot(mb_ref[...], w_ref[ka:, cs]))


def _outproj(x, mix_a, mix_b, w, layer, tm, tn):
    m, d = x.shape
    ka, kb = mix_a.shape[1], mix_b.shape[1]
    tm, tn = min(tm, m), min(tn, d)
    return pl.pallas_call(
        functools.partial(_outproj_kernel, tn),
        grid=(m // tm,),
        in_specs=[
            pl.BlockSpec((tm, d), lambda i: (i, 0)),
            pl.BlockSpec((tm, ka), lambda i: (i, 0)),
            pl.BlockSpec((tm, kb), lambda i: (i, 0)),
            _resident((None, ka + kb, d), lambda i: (layer, 0, 0)),
        ],
        out_specs=pl.BlockSpec((tm, d), lambda i: (i, 0)),
        out_shape=jax.ShapeDtypeStruct((m, d), F32),
        compiler_params=_params("parallel"),
        name="outproj",
    )(x, mix_a, mix_b, w)


def _ffn_kernel(x_ref, g_ref, wu_ref, wd_ref, o_ref, h_ref):
    @pl.when(pl.program_id(1) == 0)
    def _():
        _norm_into(h_ref, x_ref, g_ref)
        o_ref[...] = x_ref[...]

    a = _dot(h_ref[...], wu_ref[...])
    a = jnp.square(jnp.maximum(a, 0.0)).astype(BF16)
    o_ref[...] += _dot(a, wd_ref[...])


def _ffn(x, g, w_up, w_down, layer, tm, tf):
    m, d = x.shape
    f = w_up.shape[2]
    tm, tf = min(tm, m), min(tf, f)
    return pl.pallas_call(
        _ffn_kernel,
        grid=(m // tm, f // tf),
        in_specs=[
            pl.BlockSpec((tm, d), lambda i, j: (i, 0)),
            pl.BlockSpec((None, 1, d), lambda i, j: (layer, 0, 0)),
            pl.BlockSpec((None, d, tf), lambda i, j: (layer, 0, j)),
            pl.BlockSpec((None, tf, d), lambda i, j: (layer, j, 0)),
        ],
        out_specs=pl.BlockSpec((tm, d), lambda i, j: (i, 0)),
        out_shape=jax.ShapeDtypeStruct((m, d), F32),
        scratch_shapes=[pltpu.VMEM((tm, d), BF16)],
        compiler_params=_params("parallel", "arbitrary"),
        name="ffn",
    )(x, g, w_up, w_down)


def _ple_kernel(final, tn, x_ref, g_ref, wg_ref, p_ref, wp_ref, fg_ref, o_ref):
    g = g_ref[...]
    parts = _row_parts(x_ref.shape[0])
    hs = [_rms_rows(x_ref[rs, :], g).astype(BF16) for rs in parts]
    for rs, h in zip(parts, hs):
        pb = p_ref[rs, :].astype(BF16)
        for c in range(x_ref.shape[1] // tn):
            cs = slice(c * tn, (c + 1) * tn)
            gate = jax.nn.sigmoid(_dot(h, wg_ref[:, cs]))
            o_ref[rs, cs] = x_ref[rs, cs] + gate * _dot(pb, wp_ref[:, cs])
    if final:
        _norm_into(o_ref, o_ref, fg_ref)


def _ple(x, g, w_gate, p, w_proj, final_g, layer, final, tm):
    m, d = x.shape
    dp = p.shape[1]
    tm = min(tm, m)
    nblk = m // tm
    return pl.pallas_call(
        functools.partial(_ple_kernel, final, min(512, d)),
        grid=(nblk,),
        in_specs=[
            pl.BlockSpec((tm, d), lambda i: (i, 0)),
            pl.BlockSpec((None, 1, d), lambda i: (layer, 0, 0)),
            _resident((None, d, d), lambda i: (layer, 0, 0)),
            pl.BlockSpec((tm, dp), lambda i: (layer * nblk + i, 0)),
            _resident((None, dp, d), lambda i: (layer, 0, 0)),
            pl.BlockSpec((1, d), lambda i: (0, 0)),
        ],
        out_specs=pl.BlockSpec((tm, d), lambda i: (i, 0)),
        out_shape=jax.ShapeDtypeStruct((m, d), F32),
        compiler_params=_params("parallel"),
        name="ple",
    )(x, g, w_gate, p, w_proj, final_g)


def _pool_kernel(z_ref, halo_ref, pw_ref, ps_ref, o_ref):
    i = pl.program_id(1)
    tt, width = z_ref.shape[1], z_ref.shape[2]
    pg = width // len(POOL_WINDOWS)
    u = z_ref[0]
    halo = jnp.where(i == 0, 0.0, halo_ref[0])
    cur = jnp.concatenate([halo, u], axis=0)
    span, lane0 = 1, 0
    t = i * tt + lax.broadcasted_iota(jnp.int32, (tt, pg), 0)
    for gi, win in enumerate(POOL_WINDOWS):
        cur = cur[:, gi * pg - lane0:]
        lane0 = gi * pg
        while span < win:
            cur = cur + pltpu.roll(cur, span, 0)
            span *= 2
        cols = slice(gi * pg, (gi + 1) * pg)
        cnt = jnp.minimum(t + 1, win).astype(F32)
        d = cur[POOL_HALO:, :pg] / cnt - u[:, cols]
        o = _dot(d.astype(BF16), pw_ref[gi]) * ps_ref[:, cols]
        o_ref[0, :, cols] = o.astype(o_ref.dtype)


def _pool(z, pool_w, pool_scale, layer, width, tt):
    b, t, _ = z.shape
    g, pg, _ = pool_w.shape[1:]
    tt = min(tt, t)
    assert all(w & (w - 1) == 0 for w in POOL_WINDOWS) and max(POOL_WINDOWS) <= POOL_HALO
    assert list(POOL_WINDOWS) == sorted(POOL_WINDOWS) and g == len(POOL_WINDOWS) and g * pg == width
    hb = tt // POOL_HALO
    return pl.pallas_call(
        _pool_kernel,
        grid=(b, t // tt),
        in_specs=[
            pl.BlockSpec((1, tt, width), lambda bi, i: (bi, i, 0)),
            pl.BlockSpec((1, POOL_HALO, width), lambda bi, i: (bi, jnp.maximum(i * hb - 1, 0), 0)),
            pl.BlockSpec((None, g, pg, pg), lambda bi, i: (layer, 0, 0, 0)),
            pl.BlockSpec((None, 1, width), lambda bi, i: (layer, 0, 0)),
        ],
        out_specs=pl.BlockSpec((1, tt, width), lambda bi, i: (bi, i, 0)),
        out_shape=jax.ShapeDtypeStruct((b, t, width), BF16),
        compiler_params=_params("parallel", "parallel"),
        name="pool",
    )(z, z, pool_w, pool_scale)


def _prep_kernel(first, dims, zr_ref, zk_ref, zv_ref, zl_ref, hr_ref, hk_ref, hv_ref, hl_ref,
                 mu_ref, vec_ref, w1_ref, w2_ref, e_ref, tri_ref, vf_ref,
                 at_o, rt_o, bt_o, kt_o, vb_o, wl_o, bv_o, g_o, *v_o):
    width, d_w, d_g = dims
    i = pl.program_id(1)
    tt = zr_ref.shape[1]

    def shifted(z_ref, h_ref, mu):
        z = z_ref[0]
        prev = jnp.where(i == 0, 0.0, h_ref[0, SHIFT_HALO - 1:SHIFT_HALO, :])
        zp = pltpu.roll(z, 1, 0)
        row = lax.broadcasted_iota(jnp.int32, z.shape, 0)
        zp = jnp.where(row == 0, prev, zp)
        return z + (zp - z) * mu

    w0, a0, v0 = vec_ref[0:1, :], vec_ref[1:2, :], vec_ref[2:3, :]
    k_k, k_a, r_k = vec_ref[3:4, :], vec_ref[4:5, :], vec_ref[5:6, :]

    lz = shifted(zl_ref, hl_ref, mu_ref[:, 3 * width:])
    s1 = lz[:, :LANE]
    lane1 = lax.broadcasted_iota(jnp.int32, s1.shape, 1)
    act1 = jnp.where(lane1 < d_w, jnp.tanh(s1), s1)
    lo1 = _dot(act1.astype(BF16), w1_ref[...])
    s2 = lz[:, LANE:LANE + w2_ref.shape[0]]
    lane2 = lax.broadcasted_iota(jnp.int32, s2.shape, 1)
    act2 = jnp.where(lane2 < d_g, jax.nn.sigmoid(s2), s2)
    lo2 = _dot(act2.astype(BF16), w2_ref[...])
    g_o[0] = lo2[:, :width]

    lw = -EXP_NEG_HALF * jax.nn.sigmoid(w0 + lo1[:, :width])
    h1 = lw.astype(BF16)
    r1 = lw - h1.astype(F32)
    h2 = r1.astype(BF16)
    h3 = (r1 - h2.astype(F32)).astype(BF16)
    tri = tri_ref[...]
    sums = _dot(tri, h1) + _dot(tri, h2) + _dot(tri, h3)
    cum, tot = sums[:tt], sums[tt:]
    e_inv = jnp.exp(-cum)
    for c in range(tt // WKV_CHUNK):
        wl_o[0, c] = jnp.exp(tot[c * WKV_CHUNK:c * WKV_CHUNK + 1, :])

    a = jax.nn.sigmoid(a0 + lo1[:, width:])
    r = shifted(zr_ref, hr_ref, mu_ref[:, :width])
    k = shifted(zk_ref, hk_ref, mu_ref[:, width:2 * width])
    v = shifted(zv_ref, hv_ref, mu_ref[:, 2 * width:3 * width])
    if first:
        v_o[0][0] = v
    else:
        v = v + (vf_ref[0] - v) * jax.nn.sigmoid(v0 + lo2[:, width:])
    vb_o[0] = v.astype(BF16)

    x = k * k_k
    kk = x * lax.rsqrt(jnp.maximum(_segsum(x * x, e_ref[...]), 1e-24))
    k2 = k * (1.0 + (a - 1.0) * k_a)
    b = kk * a
    at_o[0] = (-kk * jnp.exp(cum - lw)).astype(BF16)
    rt_o[0] = (r * jnp.exp(cum)).astype(BF16)
    bt_o[0] = (b * e_inv).astype(BF16)
    kt_o[0] = (k2 * e_inv).astype(BF16)
    bv_o[0] = _segsum(r * k2 * r_k, e_ref[...]) * v


def _prep(z, mu, vecs, w1, w2, e, v_first, layer, first, width, d_w, d_g, tt):
    b, t, zw = z.shape
    tt = min(tt, t)
    lw = zw - 4 * width
    hb = tt // SHIFT_HALO
    lcol = 4 * width // lw
    nchunk = tt // WKV_CHUNK
    assert 4 * width % lw == 0 and tt % WKV_CHUNK == 0
    row = jnp.arange(tt)
    same_chunk = (row[:, None] // WKV_CHUNK) == (row[None, :] // WKV_CHUNK)
    tri = jnp.concatenate([same_chunk & (row[None, :] <= row[:, None]), same_chunk], axis=0).astype(BF16)

    def main(col, w):
        return pl.BlockSpec((1, tt, w), lambda bi, i: (bi, i, col))

    def halo(col, w):
        return pl.BlockSpec((1, SHIFT_HALO, w), lambda bi, i: (bi, jnp.maximum(i * hb - 1, 0), col))

    def whole(a):
        return pl.BlockSpec((None,) + a.shape[1:], lambda bi, i: (layer,) + (0,) * (a.ndim - 1))

    tile = pl.BlockSpec((1, tt, width), lambda bi, i: (bi, i, 0))
    full = lambda dt: jax.ShapeDtypeStruct((b, t, width), dt)
    out_specs = [tile] * 5 + [pl.BlockSpec((1, nchunk, 1, width), lambda bi, i: (bi, i, 0, 0)), tile, tile]
    out_shape = [full(BF16)] * 5 + [jax.ShapeDtypeStruct((b, t // WKV_CHUNK, 1, width), F32),
                                    full(F32), full(F32)]
    if first:
        out_specs.append(tile)
        out_shape.append(full(F32))
    return pl.pallas_call(
        functools.partial(_prep_kernel, first, (width, d_w, d_g)),
        grid=(b, t // tt),
        in_specs=[
            main(1, width), main(2, width), main(3, width), main(lcol, lw),
            halo(1, width), halo(2, width), halo(3, width), halo(lcol, lw),
            whole(mu), whole(vecs), whole(w1), whole(w2),
            pl.BlockSpec(e.shape, lambda bi, i: (0, 0)),
            pl.BlockSpec(tri.shape, lambda bi, i: (0, 0)),
            tile,
        ],
        out_specs=out_specs,
        out_shape=out_shape,
        compiler_params=_params("parallel", "parallel"),
        name="rwkv_prep",
    )(z, z, z, z, z, z, z, z, mu, vecs, w1, w2, e, tri, v_first)


def _wkv_kernel(at_ref, rt_ref, bt_ref, kt_ref, v_ref, wl_ref, y_ref, s_ref):
    hg = WKV_HEADS_PER_GROUP
    c = pl.program_id(1)
    nrow, L, C = at_ref.shape
    W, GL = hg * HEAD_SIZE, hg * L
    ng = C // W

    @pl.when(c == 0)
    def _():
        s_ref[...] = jnp.zeros_like(s_ref)

    lane = lax.broadcasted_iota(jnp.int32, (L, W), 1)
    head_mask = [(lane >= h * HEAD_SIZE) & (lane < (h + 1) * HEAD_SIZE) for h in range(hg)]
    row2 = lax.broadcasted_iota(jnp.int32, (L, 2 * GL), 0)
    col2 = lax.broadcasted_iota(jnp.int32, (L, 2 * GL), 1) & (L - 1)
    strict = col2 < row2
    incl = col2 <= row2
    colg = lax.broadcasted_iota(jnp.int32, (L, GL), 1)
    rowg = lax.broadcasted_iota(jnp.int32, (L, GL), 0)
    eye = (colg & (L - 1)) == rowg
    blk_mask = [(colg >= h * L) & (colg < (h + 1) * L) for h in range(hg)]
    sr = lax.broadcasted_iota(jnp.int32, (W, W), 0)
    sc = lax.broadcasted_iota(jnp.int32, (W, W), 1)
    same_head = (sr // HEAD_SIZE) == (sc // HEAD_SIZE)

    def stack_heads(x):
        return jnp.concatenate([jnp.where(m, x, 0.0).astype(BF16) for m in head_mask], axis=0)

    def block_diag(q):
        return jnp.concatenate([jnp.where(m, q, 0.0).astype(BF16) for m in blk_mask], axis=0)

    chains = [(b, g) for b in range(nrow) for g in range(ng)]
    ids = range(len(chains))

    def op(ref, i):
        b, g = chains[i]
        return ref[b, :, g * W:(g + 1) * W]

    lhs = [jnp.concatenate([op(at_ref, i), op(rt_ref, i)], axis=0) for i in ids]
    gm = [_dot_nt(lhs[i], jnp.concatenate([stack_heads(op(bt_ref, i)), stack_heads(op(kt_ref, i))],
                                          axis=0)) for i in ids]
    s0 = [s_ref[i] for i in ids]
    as0 = [_dot_nt(lhs[i], s0[i].astype(BF16)) for i in ids]
    pa = [jnp.where(strict, gm[i][:L], 0.0) for i in ids]
    pr = [jnp.where(incl, gm[i][L:], 0.0) for i in ids]
    vm = [stack_heads(op(v_ref, i)) for i in ids]

    tm = [jnp.where(eye, 1.0, pa[i][:, :GL]) for i in ids]
    q = [_dot(pa[i][:, :GL].astype(BF16), block_diag(pa[i][:, :GL])) for i in ids]
    pkv = [_dot(jnp.concatenate([pa[i][:, GL:].astype(BF16), pr[i][:, GL:].astype(BF16)], axis=0),
                vm[i]) for i in ids]
    x1 = [as0[i][:L] + pkv[i][:L] for i in ids]
    levels = L.bit_length() - 2
    for j in range(levels):
        if j + 1 < levels:
            res = [_dot(jnp.concatenate([q[i].astype(BF16), tm[i].astype(BF16)], axis=0),
                        block_diag(q[i])) for i in ids]
            q = [res[i][:L] for i in ids]
            tm = [tm[i] + res[i][L:] for i in ids]
        else:
            tm = [tm[i] + _dot(tm[i].astype(BF16), block_diag(q[i])) for i in ids]

    u = [_dot(tm[i].astype(BF16), stack_heads(x1[i])) for i in ids]
    for i, (b, g) in enumerate(chains):
        y_ref[b, :, g * W:(g + 1) * W] = (as0[i][L:] + pkv[i][L:]
                                          + _dot(pr[i][:, :GL].astype(BF16), stack_heads(u[i])))
    for i, (b, g) in enumerate(chains):
        uv = jnp.concatenate([u[i].astype(BF16), op(v_ref, i)], axis=0)
        bk = jnp.concatenate([op(bt_ref, i), op(kt_ref, i)], axis=0)
        ds = _dot_tn(uv, bk)
        wl = wl_ref[b, 0, :, g * W:(g + 1) * W]
        s_ref[i] = s0[i] * wl + jnp.where(same_head, ds, 0.0) * wl


def _wkv(at, rt, bt, kt, vb, wl):
    bsz, t, c = at.shape
    L = WKV_CHUNK
    w = WKV_HEADS_PER_GROUP * HEAD_SIZE
    assert L & (L - 1) == 0 and t % L == 0 and c % w == 0
    nrow = WKV_ROWS_PER_STEP if bsz % WKV_ROWS_PER_STEP == 0 else 1
    spec = pl.BlockSpec((nrow, L, c), lambda bi, ci: (bi, ci, 0))
    return pl.pallas_call(
        _wkv_kernel,
        grid=(bsz // nrow, t // L),
        in_specs=[spec] * 5 + [pl.BlockSpec((nrow, 1, 1, c), lambda bi, ci: (bi, ci, 0, 0))],
        out_specs=spec,
        out_shape=jax.ShapeDtypeStruct((bsz, t, c), F32),
        scratch_shapes=[pltpu.VMEM((nrow * (c // w), w, w), F32)],
        compiler_params=_params("parallel", "arbitrary"),
        name="wkv_scan",
    )(at, rt, bt, kt, vb, wl)


def _post_kernel(y_ref, bv_ref, g_ref, vec_ref, e_ref, o_ref):
    e = e_ref[...]
    gn_g, gn_b = vec_ref[0:1, :], vec_ref[1:2, :]
    inv_n = 1.0 / HEAD_SIZE
    y = y_ref[0]
    d = y - _segsum(y, e) * inv_n
    var = _segsum(d * d, e) * inv_n
    yn = d * lax.rsqrt(var + GN_EPS) * gn_g + gn_b
    o_ref[0] = ((yn + bv_ref[0]) * g_ref[0]).astype(o_ref.dtype)


def _post(y, bv, g, vecs, e, layer, tt):
    b, t, c = y.shape
    tt = min(tt, t)
    spec = pl.BlockSpec((1, tt, c), lambda bi, i: (bi, i, 0))
    return pl.pallas_call(
        _post_kernel,
        grid=(b, t // tt),
        in_specs=[spec] * 3 + [
            pl.BlockSpec((None,) + vecs.shape[1:], lambda bi, i: (layer, 0, 0)),
            pl.BlockSpec(e.shape, lambda bi, i: (0, 0)),
        ],
        out_specs=spec,
        out_shape=jax.ShapeDtypeStruct((b, t, c), BF16),
        compiler_params=_params("parallel", "parallel"),
        name="rwkv_post",
    )(y, bv, g, vecs, e)


def kernel(x, p, attn_norm, w_in, mu_shift, w_vres_dn, mu_vres, v0, v_up, pool_w, pool_scale,
           w0, w_up, a0, a_up, g_up, k_k, k_a, r_k, gn_g, gn_b, w_out, mlp_norm, w_ffn_up,
           w_ffn_down, ple_norm, w_ple_gate, w_ple_proj, final_norm):
    bsz, t, d = x.shape
    depth = w_in.shape[0]
    width = w0.shape[1]
    pool_width = pool_scale.shape[1]
    d_w, d_a, d_g, d_v = w_up.shape[1], a_up.shape[1], g_up.shape[1], v_up.shape[1]
    d_ple = p.shape[-1]
    m = bsz * t
    assert pool_width == width and d_w + d_a == LANE and width % SEG_SLAB == 0
    lora_w = 2 * SEG_SLAB
    w2_rows = lora_w - LANE - LANE
    assert d_g + d_v <= w2_rows
    zw = pool_width + 3 * width + lora_w

    n_in = w_in.shape[2]
    vres = jnp.concatenate([jnp.zeros((1, d, d_v), F32), w_vres_dn], axis=0)
    w_in_x = jnp.concatenate(
        [w_in.astype(BF16), vres.astype(BF16), jnp.zeros((depth, d, zw - n_in - d_v), BF16)], axis=2)
    mu_vres_x = jnp.concatenate([jnp.zeros((1, d_v), F32), mu_vres], axis=0)
    mu_x = jnp.concatenate(
        [mu_shift, mu_vres_x, jnp.zeros((depth, zw - n_in - d_v), F32)], axis=1)[:, None, :]
    zeros_w = lambda rows: jnp.zeros((depth, rows, width), F32)
    w1 = jnp.concatenate([
        jnp.concatenate([w_up, zeros_w(d_w)], axis=2),
        jnp.concatenate([zeros_w(d_a), a_up], axis=2)], axis=1).astype(BF16)
    v_up_x = jnp.concatenate([jnp.zeros((1, d_v, width), F32), v_up], axis=0)
    w2 = jnp.concatenate([
        jnp.concatenate([g_up, zeros_w(d_g)], axis=2),
        jnp.concatenate([zeros_w(d_v), v_up_x], axis=2),
        jnp.zeros((depth, w2_rows - d_g - d_v, 2 * width), F32)], axis=1).astype(BF16)
    v0_x = jnp.concatenate([jnp.zeros((1, width), F32), v0], axis=0)
    pad = lambda rows: jnp.zeros((depth, rows, width), F32)
    prep_vecs = jnp.concatenate(
        [jnp.stack([w0, a0, v0_x, k_k, k_a, r_k.reshape(depth, width)], axis=1), pad(2)],
        axis=1)
    post_vecs = jnp.concatenate([jnp.stack([gn_g, gn_b], axis=1), pad(6)], axis=1)
    seg = jnp.arange(SEG_SLAB) // HEAD_SIZE
    e = (seg[:, None] == seg[None, :]).astype(BF16)
    attn_g, mlp_g, ple_g = attn_norm[:, None, :], mlp_norm[:, None, :], ple_norm[:, None, :]
    pool_wb = pool_w.astype(BF16)
    pool_sc = pool_scale[:, None, :]
    w_out_b = w_out.astype(BF16)
    w_up_b, w_down_b = w_ffn_up.astype(BF16), w_ffn_down.astype(BF16)
    w_gate_b, w_proj_b = w_ple_gate.astype(BF16), w_ple_proj.astype(BF16)
    p2 = p.reshape(depth * m, d_ple)
    final_g = final_norm[None, :]

    xf = x.reshape(m, d)
    v_first = None
    for i in range(depth):
        z = _rms_matmul(xf, attn_g, w_in_x, i, tm=512, tn=768).reshape(bsz, t, zw)
        pool_out = _pool(z, pool_wb, pool_sc, i, pool_width, tt=512)
        first = i == 0
        at, rt, bt, kt, vb, wl, bv, g, *v_new = _prep(
            z, mu_x, prep_vecs, w1, w2, e, z if first else v_first, i, first, width, d_w, d_g, tt=256)
        if first:
            v_first = v_new[0]
        y = _wkv(at, rt, bt, kt, vb, wl)
        rwkv_out = _post(y, bv, g, post_vecs, e, i, tt=512)
        xf = _outproj(xf, pool_out.reshape(m, pool_width), rwkv_out.reshape(m, width), w_out_b, i,
                      tm=512, tn=512)
        xf = _ffn(xf, mlp_g, w_up_b, w_down_b, i, tm=512, tf=1024)
        xf = _ple(xf, ple_g, w_gate_b, p2, w_proj_b, final_g, i, i == depth - 1, tm=512)
    return xf.reshape(bsz, t, d)
```

```python
import functools

import jax
import jax.numpy as jnp
from jax import lax
from jax.experimental import pallas as pl
from jax.experimental.pallas import tpu as pltpu

F32 = jnp.float32
BF16 = jnp.bfloat16

NORM_EPS = 1e-6
GN_EPS = 64e-5
EXP_NEG_HALF = 0.6065306597126334
HEAD_SIZE = 64
POOL_WINDOWS = (2, 4, 8, 16)
POOL_HALO = 16
SHIFT_HALO = 8
WKV_CHUNK = 64
WKV_HEADS_PER_GROUP = 2
WKV_ROWS_PER_STEP = 4
SEG_SLAB = 256
LANE = 128
VMEM_LIMIT = 52 * 1024 * 1024


def _dot(a, b):
    return jnp.dot(a, b, preferred_element_type=F32)


def _dot_nt(a, b):
    return lax.dot_general(a, b, (((1,), (1,)), ((), ())), preferred_element_type=F32)


def _dot_tn(a, b):
    return lax.dot_general(a, b, (((0,), (0,)), ((), ())), preferred_element_type=F32)


def _params(*sem):
    return pltpu.CompilerParams(dimension_semantics=sem, vmem_limit_bytes=VMEM_LIMIT)


def _rms_rows(x, g):
    ms = jnp.mean(x * x, axis=-1, keepdims=True)
    return x * lax.rsqrt(ms + NORM_EPS) * g


def _norm_into(h_ref, x_ref, g_ref, rows=256):
    rows = min(rows, x_ref.shape[0])
    n = x_ref.shape[0] // rows

    def body(c, carry):
        rs = pl.ds(pl.multiple_of(c * rows, rows), rows)
        h_ref[rs, :] = _rms_rows(x_ref[rs, :], g_ref[...]).astype(h_ref.dtype)
        return carry

    lax.fori_loop(0, n, body, 0)


def _segsum(x, e):
    outs = []
    for s in range(x.shape[1] // SEG_SLAB):
        xs = x[:, s * SEG_SLAB:(s + 1) * SEG_SLAB]
        hi = xs.astype(BF16)
        lo = (xs - hi.astype(F32)).astype(BF16)
        outs.append(_dot(hi, e) + _dot(lo, e))
    return jnp.concatenate(outs, axis=1)


def _row_parts(n_rows, parts=2):
    step = n_rows // parts
    return [slice(s * step, (s + 1) * step) for s in range(parts)]


def _rms_matmul_kernel(tn, x_ref, g_ref, w_ref, o_ref):
    g = g_ref[...]
    parts = _row_parts(x_ref.shape[0])
    hs = [_rms_rows(x_ref[rs, :], g).astype(BF16) for rs in parts]
    for rs, h in zip(parts, hs):
        for c in range(w_ref.shape[1] // tn):
            cs = slice(c * tn, (c + 1) * tn)
            o_ref[rs, cs] = _dot(h, w_ref[:, cs])


def _resident(block_shape, index_map):
    return pl.BlockSpec(block_shape, index_map, pipeline_mode=pl.Buffered(1))


def _rms_matmul(x, g, w, layer, tm, tn):
    m, d = x.shape
    n = w.shape[2]
    tm, tn = min(tm, m), min(tn, n)
    return pl.pallas_call(
        functools.partial(_rms_matmul_kernel, tn),
        grid=(m // tm,),
        in_specs=[
            pl.BlockSpec((tm, d), lambda i: (i, 0)),
            pl.BlockSpec((None, 1, d), lambda i: (layer, 0, 0)),
            _resident((None, d, n), lambda i: (layer, 0, 0)),
        ],
        out_specs=pl.BlockSpec((tm, n), lambda i: (i, 0)),
        out_shape=jax.ShapeDtypeStruct((m, n), F32),
        compiler_params=_params("parallel"),
        name="rms_matmul",
    )(x, g, w)


def _outproj_kernel(tn, x_ref, ma_ref, mb_ref, w_ref, o_ref):
    ka = ma_ref.shape[1]
    for c in range(x_ref.shape[1] // tn):
        cs = slice(c * tn, (c + 1) * tn)
        o_ref[:, cs] = (x_ref[:, cs] + _dot(ma_ref[...], w_ref[:ka, cs])
                        + _dot(mb_ref[...], w_ref[ka:, cs]))


def _outproj(x, mix_a, mix_b, w, layer, tm, tn):
    m, d = x.shape
    ka, kb = mix_a.shape[1], mix_b.shape[1]
    tm, tn = min(tm, m), min(tn, d)
    return pl.pallas_call(
        functools.partial(_outproj_kernel, tn),
        grid=(m // tm,),
        in_specs=[
            pl.BlockSpec((tm, d), lambda i: (i, 0)),
            pl.BlockSpec((tm, ka), lambda i: (i, 0)),
            pl.BlockSpec((tm, kb), lambda i: (i, 0)),
            _resident((None, ka + kb, d), lambda i: (layer, 0, 0)),
        ],
        out_specs=pl.BlockSpec((tm, d), lambda i: (i, 0)),
        out_shape=jax.ShapeDtypeStruct((m, d), F32),
        compiler_params=_params("parallel"),
        name="outproj",
    )(x, mix_a, mix_b, w)


def _ffn_kernel(x_ref, g_ref, wu_ref, wd_ref, o_ref, h_ref):
    @pl.when(pl.program_id(1) == 0)
    def _():
        _norm_into(h_ref, x_ref, g_ref)
        o_ref[...] = x_ref[...]

    a = _dot(h_ref[...], wu_ref[...])
    a = jnp.square(jnp.maximum(a, 0.0)).astype(BF16)
    o_ref[...] += _dot(a, wd_ref[...])


def _ffn(x, g, w_up, w_down, layer, tm, tf):
    m, d = x.shape
    f = w_up.shape[1]
    tm, tf = min(tm, m), min(tf, f)
    return pl.pallas_call(
        _ffn_kernel,
        grid=(m // tm, f // tf),
        in_specs=[
            pl.BlockSpec((tm, d), lambda i, j: (i, 0)),
            pl.BlockSpec((None, 1, d), lambda i, j: (layer, 0, 0)),
            pl.BlockSpec((d, tf), lambda i, j: (0, j)),
            pl.BlockSpec((tf, d), lambda i, j: (j, 0)),
        ],
        out_specs=pl.BlockSpec((tm, d), lambda i, j: (i, 0)),
        out_shape=jax.ShapeDtypeStruct((m, d), F32),
        scratch_shapes=[pltpu.VMEM((tm, d), BF16)],
        compiler_params=_params("parallel", "arbitrary"),
        name="ffn",
    )(x, g, w_up, w_down)


def _ple_kernel(final, tn, x_ref, g_ref, wg_ref, p_ref, wp_ref, fg_ref, o_ref):
    g = g_ref[...]
    parts = _row_parts(x_ref.shape[0])
    hs = [_rms_rows(x_ref[rs, :], g).astype(BF16) for rs in parts]
    for rs, h in zip(parts, hs):
        pb = p_ref[rs, :].astype(BF16)
        for c in range(x_ref.shape[1] // tn):
            cs = slice(c * tn, (c + 1) * tn)
            gate = jax.nn.sigmoid(_dot(h, wg_ref[:, cs]))
            o_ref[rs, cs] = x_ref[rs, cs] + gate * _dot(pb, wp_ref[:, cs])
    if final:
        _norm_into(o_ref, o_ref, fg_ref)


def _ple(x, g, w_gate, p, w_proj, final_g, layer, final, tm):
    m, d = x.shape
    dp = p.shape[1]
    tm = min(tm, m)
    nblk = m // tm
    return pl.pallas_call(
        functools.partial(_ple_kernel, final, min(512, d)),
        grid=(nblk,),
        in_specs=[
            pl.BlockSpec((tm, d), lambda i: (i, 0)),
            pl.BlockSpec((None, 1, d), lambda i: (layer, 0, 0)),
            _resident((None, d, d), lambda i: (layer, 0, 0)),
            pl.BlockSpec((tm, dp), lambda i: (layer * nblk + i, 0)),
            _resident((None, dp, d), lambda i: (layer, 0, 0)),
            pl.BlockSpec((1, d), lambda i: (0, 0)),
        ],
        out_specs=pl.BlockSpec((tm, d), lambda i: (i, 0)),
        out_shape=jax.ShapeDtypeStruct((m, d), F32),
        compiler_params=_params("parallel"),
        name="ple",
    )(x, g, w_gate, p, w_proj, final_g)


def _pool_kernel(z_ref, halo_ref, pw_ref, ps_ref, o_ref):
    i = pl.program_id(1)
    tt, width = z_ref.shape[1], z_ref.shape[2]
    pg = width // len(POOL_WINDOWS)
    u = z_ref[0]
    halo = jnp.where(i == 0, 0.0, halo_ref[0])
    cur = jnp.concatenate([halo, u], axis=0)
    span, lane0 = 1, 0
    t = i * tt + lax.broadcasted_iota(jnp.int32, (tt, pg), 0)
    for gi, win in enumerate(POOL_WINDOWS):
        cur = cur[:, gi * pg - lane0:]
        lane0 = gi * pg
        while span < win:
            cur = cur + pltpu.roll(cur, span, 0)
            span *= 2
        cols = slice(gi * pg, (gi + 1) * pg)
        cnt = jnp.minimum(t + 1, win).astype(F32)
        d = cur[POOL_HALO:, :pg] / cnt - u[:, cols]
        o = _dot(d.astype(BF16), pw_ref[gi]) * ps_ref[:, cols]
        o_ref[0, :, cols] = o.astype(o_ref.dtype)


def _pool(z, pool_w, pool_scale, layer, width, tt):
    b, t, _ = z.shape
    g, pg, _ = pool_w.shape[1:]
    tt = min(tt, t)
    assert all(w & (w - 1) == 0 for w in POOL_WINDOWS) and max(POOL_WINDOWS) <= POOL_HALO
    assert list(POOL_WINDOWS) == sorted(POOL_WINDOWS) and g == len(POOL_WINDOWS) and g * pg == width
    hb = tt // POOL_HALO
    return pl.pallas_call(
        _pool_kernel,
        grid=(b, t // tt),
        in_specs=[
            pl.BlockSpec((1, tt, width), lambda bi, i: (bi, i, 0)),
            pl.BlockSpec((1, POOL_HALO, width), lambda bi, i: (bi, jnp.maximum(i * hb - 1, 0), 0)),
            pl.BlockSpec((None, g, pg, pg), lambda bi, i: (layer, 0, 0, 0)),
            pl.BlockSpec((None, 1, width), lambda bi, i: (layer, 0, 0)),
        ],
        out_specs=pl.BlockSpec((1, tt, width), lambda bi, i: (bi, i, 0)),
        out_shape=jax.ShapeDtypeStruct((b, t, width), BF16),
        compiler_params=_params("parallel", "parallel"),
        name="pool",
    )(z, z, pool_w, pool_scale)


def _prep_kernel(first, dims, zr_ref, zk_ref, zv_ref, zl_ref, hr_ref, hk_ref, hv_ref, hl_ref,
                 mu_ref, vec_ref, w1_ref, w2_ref, e_ref, tri_ref, vf_ref,
                 at_o, rt_o, bt_o, kt_o, vb_o, wl_o, bv_o, g_o, *v_o):
    width, d_w, d_g = dims
    i = pl.program_id(1)
    tt = zr_ref.shape[1]

    def shifted(z_ref, h_ref, mu):
        z = z_ref[0]
        prev = jnp.where(i == 0, 0.0, h_ref[0, SHIFT_HALO - 1:SHIFT_HALO, :])
        zp = pltpu.roll(z, 1, 0)
        row = lax.broadcasted_iota(jnp.int32, z.shape, 0)
        zp = jnp.where(row == 0, prev, zp)
        return z + (zp - z) * mu

    w0, a0, v0 = vec_ref[0:1, :], vec_ref[1:2, :], vec_ref[2:3, :]
    k_k, k_a, r_k = vec_ref[3:4, :], vec_ref[4:5, :], vec_ref[5:6, :]

    lz = shifted(zl_ref, hl_ref, mu_ref[:, 3 * width:])
    s1 = lz[:, :LANE]
    lane1 = lax.broadcasted_iota(jnp.int32, s1.shape, 1)
    act1 = jnp.where(lane1 < d_w, jnp.tanh(s1), s1)
    lo1 = _dot(act1.astype(BF16), w1_ref[...])
    s2 = lz[:, LANE:LANE + w2_ref.shape[0]]
    lane2 = lax.broadcasted_iota(jnp.int32, s2.shape, 1)
    act2 = jnp.where(lane2 < d_g, jax.nn.sigmoid(s2), s2)
    lo2 = _dot(act2.astype(BF16), w2_ref[...])
    g_o[0] = lo2[:, :width]

    lw = -EXP_NEG_HALF * jax.nn.sigmoid(w0 + lo1[:, :width])
    h1 = lw.astype(BF16)
    r1 = lw - h1.astype(F32)
    h2 = r1.astype(BF16)
    h3 = (r1 - h2.astype(F32)).astype(BF16)
    tri = tri_ref[...]
    sums = _dot(tri, h1) + _dot(tri, h2) + _dot(tri, h3)
    cum, tot = sums[:tt], sums[tt:]
    e_inv = jnp.exp(-cum)
    for c in range(tt // WKV_CHUNK):
        wl_o[0, c] = jnp.exp(tot[c * WKV_CHUNK:c * WKV_CHUNK + 1, :])

    a = jax.nn.sigmoid(a0 + lo1[:, width:])
    r = shifted(zr_ref, hr_ref, mu_ref[:, :width])
    k = shifted(zk_ref, hk_ref, mu_ref[:, width:2 * width])
    v = shifted(zv_ref, hv_ref, mu_ref[:, 2 * width:3 * width])
    if first:
        v_o[0][0] = v
    else:
        v = v + (vf_ref[0] - v) * jax.nn.sigmoid(v0 + lo2[:, width:])
    vb_o[0] = v.astype(BF16)

    x = k * k_k
    kk = x * lax.rsqrt(jnp.maximum(_segsum(x * x, e_ref[...]), 1e-24))
    k2 = k * (1.0 + (a - 1.0) * k_a)
    b = kk * a
    at_o[0] = (-kk * jnp.exp(cum - lw)).astype(BF16)
    rt_o[0] = (r * jnp.exp(cum)).astype(BF16)
    bt_o[0] = (b * e_inv).astype(BF16)
    kt_o[0] = (k2 * e_inv).astype(BF16)
    bv_o[0] = _segsum(r * k2 * r_k, e_ref[...]) * v


def _prep(z, mu, vecs, w1, w2, e, v_first, layer, first, width, d_w, d_g, tt):
    b, t, zw = z.shape
    tt = min(tt, t)
    lw = zw - 4 * width
    hb = tt // SHIFT_HALO
    lcol = 4 * width // lw
    nchunk = tt // WKV_CHUNK
    assert 4 * width % lw == 0 and tt % WKV_CHUNK == 0
    row = jnp.arange(tt)
    same_chunk = (row[:, None] // WKV_CHUNK) == (row[None, :] // WKV_CHUNK)
    tri = jnp.concatenate([same_chunk & (row[None, :] <= row[:, None]), same_chunk], axis=0).astype(BF16)

    def main(col, w):
        return pl.BlockSpec((1, tt, w), lambda bi, i: (bi, i, col))

    def halo(col, w):
        return pl.BlockSpec((1, SHIFT_HALO, w), lambda bi, i: (bi, jnp.maximum(i * hb - 1, 0), col))

    def whole(a):
        return pl.BlockSpec((None,) + a.shape[1:], lambda bi, i: (layer,) + (0,) * (a.ndim - 1))

    tile = pl.BlockSpec((1, tt, width), lambda bi, i: (bi, i, 0))
    full = lambda dt: jax.ShapeDtypeStruct((b, t, width), dt)
    out_specs = [tile] * 5 + [pl.BlockSpec((1, nchunk, 1, width), lambda bi, i: (bi, i, 0, 0)), tile, tile]
    out_shape = [full(BF16)] * 5 + [jax.ShapeDtypeStruct((b, t // WKV_CHUNK, 1, width), F32),
                                    full(F32), full(F32)]
    if first:
        out_specs.append(tile)
        out_shape.append(full(F32))
    return pl.pallas_call(
        functools.partial(_prep_kernel, first, (width, d_w, d_g)),
        grid=(b, t // tt),
        in_specs=[
            main(1, width), main(2, width), main(3, width), main(lcol, lw),
            halo(1, width), halo(2, width), halo(3, width), halo(lcol, lw),
            whole(mu), whole(vecs), whole(w1), whole(w2),
            pl.BlockSpec(e.shape, lambda bi, i: (0, 0)),
            pl.BlockSpec(tri.shape, lambda bi, i: (0, 0)),
            tile,
        ],
        out_specs=out_specs,
        out_shape=out_shape,
        compiler_params=_params("parallel", "parallel"),
        name="rwkv_prep",
    )(z, z, z, z, z, z, z, z, mu, vecs, w1, w2, e, tri, v_first)


def _wkv_kernel(at_ref, rt_ref, bt_ref, kt_ref, v_ref, wl_ref, wu_ref, wd_ref,
                y_ref, wu_o, wd_o, s_ref):
    hg = WKV_HEADS_PER_GROUP
    c = pl.program_id(1)
    nrow, L, C = at_ref.shape
    W, GL = hg * HEAD_SIZE, hg * L
    ng = C // W

    wu_o[...] = wu_ref[...].astype(BF16)
    wd_o[...] = wd_ref[...].astype(BF16)

    @pl.when(c == 0)
    def _():
        s_ref[...] = jnp.zeros_like(s_ref)

    lane = lax.broadcasted_iota(jnp.int32, (L, W), 1)
    head_mask = [(lane >= h * HEAD_SIZE) & (lane < (h + 1) * HEAD_SIZE) for h in range(hg)]
    row2 = lax.broadcasted_iota(jnp.int32, (L, 2 * GL), 0)
    col2 = lax.broadcasted_iota(jnp.int32, (L, 2 * GL), 1) & (L - 1)
    strict = col2 < row2
    incl = col2 <= row2
    colg = lax.broadcasted_iota(jnp.int32, (L, GL), 1)
    rowg = lax.broadcasted_iota(jnp.int32, (L, GL), 0)
    eye = (colg & (L - 1)) == rowg
    blk_mask = [(colg >= h * L) & (colg < (h + 1) * L) for h in range(hg)]
    sr = lax.broadcasted_iota(jnp.int32, (W, W), 0)
    sc = lax.broadcasted_iota(jnp.int32, (W, W), 1)
    same_head = (sr // HEAD_SIZE) == (sc // HEAD_SIZE)

    def stack_heads(x):
        return jnp.concatenate([jnp.where(m, x, 0.0).astype(BF16) for m in head_mask], axis=0)

    def block_diag(q):
        return jnp.concatenate([jnp.where(m, q, 0.0).astype(BF16) for m in blk_mask], axis=0)

    chains = [(b, g) for b in range(nrow) for g in range(ng)]
    ids = range(len(chains))

    def op(ref, i):
        b, g = chains[i]
        return ref[b, :, g * W:(g + 1) * W]

    lhs = [jnp.concatenate([op(at_ref, i), op(rt_ref, i)], axis=0) for i in ids]
    gm = [_dot_nt(lhs[i], jnp.concatenate([stack_heads(op(bt_ref, i)), stack_heads(op(kt_ref, i))],
                                          axis=0)) for i in ids]
    s0 = [s_ref[i] for i in ids]
    as0 = [_dot_nt(lhs[i], s0[i].astype(BF16)) for i in ids]
    pa = [jnp.where(strict, gm[i][:L], 0.0) for i in ids]
    pr = [jnp.where(incl, gm[i][L:], 0.0) for i in ids]
    vm = [stack_heads(op(v_ref, i)) for i in ids]

    tm = [jnp.where(eye, 1.0, pa[i][:, :GL]) for i in ids]
    q = [_dot(pa[i][:, :GL].astype(BF16), block_diag(pa[i][:, :GL])) for i in ids]
    pkv = [_dot(jnp.concatenate([pa[i][:, GL:].astype(BF16), pr[i][:, GL:].astype(BF16)], axis=0),
                vm[i]) for i in ids]
    x1 = [as0[i][:L] + pkv[i][:L] for i in ids]
    levels = L.bit_length() - 2
    for j in range(levels):
        if j + 1 < levels:
            res = [_dot(jnp.concatenate([q[i].astype(BF16), tm[i].astype(BF16)], axis=0),
                        block_diag(q[i])) for i in ids]
            q = [res[i][:L] for i in ids]
            tm = [tm[i] + res[i][L:] for i in ids]
        else:
            tm = [tm[i] + _dot(tm[i].astype(BF16), block_diag(q[i])) for i in ids]

    u = [_dot(tm[i].astype(BF16), stack_heads(x1[i])) for i in ids]
    for i, (b, g) in enumerate(chains):
        y_ref[b, :, g * W:(g + 1) * W] = (as0[i][L:] + pkv[i][L:]
                                          + _dot(pr[i][:, :GL].astype(BF16), stack_heads(u[i])))
    for i, (b, g) in enumerate(chains):
        uv = jnp.concatenate([u[i].astype(BF16), op(v_ref, i)], axis=0)
        bk = jnp.concatenate([op(bt_ref, i), op(kt_ref, i)], axis=0)
        ds = _dot_tn(uv, bk)
        wl = wl_ref[b, 0, :, g * W:(g + 1) * W]
        s_ref[i] = s0[i] * wl + jnp.where(same_head, ds, 0.0) * wl


def _wkv(at, rt, bt, kt, vb, wl, w_up, w_down, layer):
    bsz, t, c = at.shape
    L = WKV_CHUNK
    w = WKV_HEADS_PER_GROUP * HEAD_SIZE
    assert L & (L - 1) == 0 and t % L == 0 and c % w == 0
    nrow = WKV_ROWS_PER_STEP if bsz % WKV_ROWS_PER_STEP == 0 else 1
    steps = t // L
    slabs = (bsz // nrow) * steps
    _, d, f = w_up.shape
    assert d % slabs == 0 and f % slabs == 0
    spec = pl.BlockSpec((nrow, L, c), lambda bi, ci: (bi, ci, 0))
    up_in = pl.BlockSpec((None, d // slabs, f), lambda bi, ci: (layer, bi * steps + ci, 0))
    down_in = pl.BlockSpec((None, f // slabs, d), lambda bi, ci: (layer, bi * steps + ci, 0))
    up_out = pl.BlockSpec((d // slabs, f), lambda bi, ci: (bi * steps + ci, 0))
    down_out = pl.BlockSpec((f // slabs, d), lambda bi, ci: (bi * steps + ci, 0))
    return pl.pallas_call(
        _wkv_kernel,
        grid=(bsz // nrow, steps),
        in_specs=[spec] * 5 + [pl.BlockSpec((nrow, 1, 1, c), lambda bi, ci: (bi, ci, 0, 0)),
                               up_in, down_in],
        out_specs=[spec, up_out, down_out],
        out_shape=[jax.ShapeDtypeStruct((bsz, t, c), F32),
                   jax.ShapeDtypeStruct((d, f), BF16), jax.ShapeDtypeStruct((f, d), BF16)],
        scratch_shapes=[pltpu.VMEM((nrow * (c // w), w, w), F32)],
        compiler_params=_params("parallel", "arbitrary"),
        name="wkv_scan",
    )(at, rt, bt, kt, vb, wl, w_up, w_down)


def _post_kernel(y_ref, bv_ref, g_ref, vec_ref, e_ref, o_ref):
    e = e_ref[...]
    gn_g, gn_b = vec_ref[0:1, :], vec_ref[1:2, :]
    inv_n = 1.0 / HEAD_SIZE
    y = y_ref[0]
    d = y - _segsum(y, e) * inv_n
    var = _segsum(d * d, e) * inv_n
    yn = d * lax.rsqrt(var + GN_EPS) * gn_g + gn_b
    o_ref[0] = ((yn + bv_ref[0]) * g_ref[0]).astype(o_ref.dtype)


def _post(y, bv, g, vecs, e, layer, tt):
    b, t, c = y.shape
    tt = min(tt, t)
    spec = pl.BlockSpec((1, tt, c), lambda bi, i: (bi, i, 0))
    return pl.pallas_call(
        _post_kernel,
        grid=(b, t // tt),
        in_specs=[spec] * 3 + [
            pl.BlockSpec((None,) + vecs.shape[1:], lambda bi, i: (layer, 0, 0)),
            pl.BlockSpec(e.shape, lambda bi, i: (0, 0)),
        ],
        out_specs=spec,
        out_shape=jax.ShapeDtypeStruct((b, t, c), BF16),
        compiler_params=_params("parallel", "parallel"),
        name="rwkv_post",
    )(y, bv, g, vecs, e)


def kernel(x, p, attn_norm, w_in, mu_shift, w_vres_dn, mu_vres, v0, v_up, pool_w, pool_scale,
           w0, w_up, a0, a_up, g_up, k_k, k_a, r_k, gn_g, gn_b, w_out, mlp_norm, w_ffn_up,
           w_ffn_down, ple_norm, w_ple_gate, w_ple_proj, final_norm):
    bsz, t, d = x.shape
    depth = w_in.shape[0]
    width = w0.shape[1]
    pool_width = pool_scale.shape[1]
    d_w, d_a, d_g, d_v = w_up.shape[1], a_up.shape[1], g_up.shape[1], v_up.shape[1]
    d_ple = p.shape[-1]
    m = bsz * t
    assert pool_width == width and d_w + d_a == LANE and width % SEG_SLAB == 0
    lora_w = 2 * SEG_SLAB
    w2_rows = lora_w - LANE - LANE
    assert d_g + d_v <= w2_rows
    zw = pool_width + 3 * width + lora_w

    n_in = w_in.shape[2]
    vres = jnp.concatenate([jnp.zeros((1, d, d_v), F32), w_vres_dn], axis=0)
    w_in_x = jnp.concatenate(
        [w_in.astype(BF16), vres.astype(BF16), jnp.zeros((depth, d, zw - n_in - d_v), BF16)], axis=2)
    mu_vres_x = jnp.concatenate([jnp.zeros((1, d_v), F32), mu_vres], axis=0)
    mu_x = jnp.concatenate(
        [mu_shift, mu_vres_x, jnp.zeros((depth, zw - n_in - d_v), F32)], axis=1)[:, None, :]
    zeros_w = lambda rows: jnp.zeros((depth, rows, width), F32)
    w1 = jnp.concatenate([
        jnp.concatenate([w_up, zeros_w(d_w)], axis=2),
        jnp.concatenate([zeros_w(d_a), a_up], axis=2)], axis=1).astype(BF16)
    v_up_x = jnp.concatenate([jnp.zeros((1, d_v, width), F32), v_up], axis=0)
    w2 = jnp.concatenate([
        jnp.concatenate([g_up, zeros_w(d_g)], axis=2),
        jnp.concatenate([zeros_w(d_v), v_up_x], axis=2),
        jnp.zeros((depth, w2_rows - d_g - d_v, 2 * width), F32)], axis=1).astype(BF16)
    v0_x = jnp.concatenate([jnp.zeros((1, width), F32), v0], axis=0)
    pad = lambda rows: jnp.zeros((depth, rows, width), F32)
    prep_vecs = jnp.concatenate(
        [jnp.stack([w0, a0, v0_x, k_k, k_a, r_k.reshape(depth, width)], axis=1), pad(2)],
        axis=1)
    post_vecs = jnp.concatenate([jnp.stack([gn_g, gn_b], axis=1), pad(6)], axis=1)
    seg = jnp.arange(SEG_SLAB) // HEAD_SIZE
    e = (seg[:, None] == seg[None, :]).astype(BF16)
    attn_g, mlp_g, ple_g = attn_norm[:, None, :], mlp_norm[:, None, :], ple_norm[:, None, :]
    pool_wb = pool_w.astype(BF16)
    pool_sc = pool_scale[:, None, :]
    w_out_b = w_out.astype(BF16)
    w_gate_b, w_proj_b = w_ple_gate.astype(BF16), w_ple_proj.astype(BF16)
    p2 = p.reshape(depth * m, d_ple)
    final_g = final_norm[None, :]

    xf = x.reshape(m, d)
    v_first = None
    for i in range(depth):
        z = _rms_matmul(xf, attn_g, w_in_x, i, tm=512, tn=768).reshape(bsz, t, zw)
        pool_out = _pool(z, pool_wb, pool_sc, i, pool_width, tt=512)
        first = i == 0
        at, rt, bt, kt, vb, wl, bv, g, *v_new = _prep(
            z, mu_x, prep_vecs, w1, w2, e, z if first else v_first, i, first, width, d_w, d_g, tt=256)
        if first:
            v_first = v_new[0]
        y, w_up_b, w_down_b = _wkv(at, rt, bt, kt, vb, wl, w_ffn_up, w_ffn_down, i)
        rwkv_out = _post(y, bv, g, post_vecs, e, i, tt=512)
        xf = _outproj(xf, pool_out.reshape(m, pool_width), rwkv_out.reshape(m, width), w_out_b, i,
                      tm=512, tn=512)
        xf = _ffn(xf, mlp_g, w_up_b, w_down_b, i, tm=512, tf=1024)
        xf = _ple(xf, ple_g, w_gate_b, p2, w_proj_b, final_g, i, i == depth - 1, tm=512)
    return xf.reshape(bsz, t, d)
```

```python
import functools

import jax
import jax.numpy as jnp
from jax import lax
from jax.experimental import pallas as pl
from jax.experimental.pallas import tpu as pltpu

F32 = jnp.float32
BF16 = jnp.bfloat16

NORM_EPS = 1e-6
GN_EPS = 64e-5
EXP_NEG_HALF = 0.6065306597126334
HEAD_SIZE = 64
POOL_WINDOWS = (2, 4, 8, 16)
POOL_HALO = 16
SHIFT_HALO = 8
WKV_CHUNK = 64
WKV_HEADS_PER_GROUP = 2
WKV_ROWS_PER_STEP = 4
SEG_SLAB = 256
LANE = 128
VMEM_LIMIT = 52 * 1024 * 1024


def _dot(a, b):
    return jnp.dot(a, b, preferred_element_type=F32)


def _dot_nt(a, b):
    return lax.dot_general(a, b, (((1,), (1,)), ((), ())), preferred_element_type=F32)


def _dot_tn(a, b):
    return lax.dot_general(a, b, (((0,), (0,)), ((), ())), preferred_element_type=F32)


def _params(*sem):
    return pltpu.CompilerParams(dimension_semantics=sem, vmem_limit_bytes=VMEM_LIMIT)


def _rms_rows(x, g):
    ms = jnp.mean(x * x, axis=-1, keepdims=True)
    return x * lax.rsqrt(ms + NORM_EPS) * g


def _norm_into(h_ref, x_ref, g_ref, rows=256):
    rows = min(rows, x_ref.shape[0])
    n = x_ref.shape[0] // rows

    def body(c, carry):
        rs = pl.ds(pl.multiple_of(c * rows, rows), rows)
        h_ref[rs, :] = _rms_rows(x_ref[rs, :], g_ref[...]).astype(h_ref.dtype)
        return carry

    lax.fori_loop(0, n, body, 0)


def _segsum(x, e):
    outs = []
    for s in range(x.shape[1] // SEG_SLAB):
        xs = x[:, s * SEG_SLAB:(s + 1) * SEG_SLAB]
        hi = xs.astype(BF16)
        lo = (xs - hi.astype(F32)).astype(BF16)
        outs.append(_dot(hi, e) + _dot(lo, e))
    return jnp.concatenate(outs, axis=1)


def _row_parts(n_rows, parts=2):
    step = n_rows // parts
    return [slice(s * step, (s + 1) * step) for s in range(parts)]


def _rms_matmul_kernel(tn, x_ref, g_ref, w_ref, o_ref):
    g = g_ref[...]
    parts = _row_parts(x_ref.shape[0])
    hs = [_rms_rows(x_ref[rs, :], g).astype(BF16) for rs in parts]
    for rs, h in zip(parts, hs):
        for c in range(w_ref.shape[1] // tn):
            cs = slice(c * tn, (c + 1) * tn)
            o_ref[rs, cs] = _dot(h, w_ref[:, cs])


def _resident(block_shape, index_map):
    return pl.BlockSpec(block_shape, index_map, pipeline_mode=pl.Buffered(1))


def _rms_matmul(x, g, w, layer, tm, tn):
    m, d = x.shape
    n = w.shape[1]
    tm, tn = min(tm, m), min(tn, n)
    return pl.pallas_call(
        functools.partial(_rms_matmul_kernel, tn),
        grid=(m // tm,),
        in_specs=[
            pl.BlockSpec((tm, d), lambda i: (i, 0)),
            pl.BlockSpec((None, 1, d), lambda i: (layer, 0, 0)),
            _resident((d, n), lambda i: (0, 0)),
        ],
        out_specs=pl.BlockSpec((tm, n), lambda i: (i, 0)),
        out_shape=jax.ShapeDtypeStruct((m, n), F32),
        compiler_params=_params("parallel"),
        name="rms_matmul",
    )(x, g, w)


def _outproj_kernel(tn, x_ref, ma_ref, mb_ref, w_ref, o_ref):
    ka = ma_ref.shape[1]
    for c in range(x_ref.shape[1] // tn):
        cs = slice(c * tn, (c + 1) * tn)
        o_ref[:, cs] = (x_ref[:, cs] + _dot(ma_ref[...], w_ref[:ka, cs])
                        + _dot(mb_ref[...], w_ref[ka:, cs]))


def _outproj(x, mix_a, mix_b, w, tm, tn):
    m, d = x.shape
    ka, kb = mix_a.shape[1], mix_b.shape[1]
    tm, tn = min(tm, m), min(tn, d)
    return pl.pallas_call(
        functools.partial(_outproj_kernel, tn),
        grid=(m // tm,),
        in_specs=[
            pl.BlockSpec((tm, d), lambda i: (i, 0)),
            pl.BlockSpec((tm, ka), lambda i: (i, 0)),
            pl.BlockSpec((tm, kb), lambda i: (i, 0)),
            _resident((ka + kb, d), lambda i: (0, 0)),
        ],
        out_specs=pl.BlockSpec((tm, d), lambda i: (i, 0)),
        out_shape=jax.ShapeDtypeStruct((m, d), F32),
        compiler_params=_params("parallel"),
        name="outproj",
    )(x, mix_a, mix_b, w)


def _ffn_kernel(x_ref, g_ref, wu_ref, wd_ref, o_ref, h_ref):
    @pl.when(pl.program_id(1) == 0)
    def _():
        _norm_into(h_ref, x_ref, g_ref)
        o_ref[...] = x_ref[...]

    a = _dot(h_ref[...], wu_ref[...])
    a = jnp.square(jnp.maximum(a, 0.0)).astype(BF16)
    o_ref[...] += _dot(a, wd_ref[...])


def _ffn(x, g, w_up, w_down, layer, tm, tf):
    m, d = x.shape
    f = w_up.shape[1]
    tm, tf = min(tm, m), min(tf, f)
    return pl.pallas_call(
        _ffn_kernel,
        grid=(m // tm, f // tf),
        in_specs=[
            pl.BlockSpec((tm, d), lambda i, j: (i, 0)),
            pl.BlockSpec((None, 1, d), lambda i, j: (layer, 0, 0)),
            pl.BlockSpec((d, tf), lambda i, j: (0, j)),
            pl.BlockSpec((tf, d), lambda i, j: (j, 0)),
        ],
        out_specs=pl.BlockSpec((tm, d), lambda i, j: (i, 0)),
        out_shape=jax.ShapeDtypeStruct((m, d), F32),
        scratch_shapes=[pltpu.VMEM((tm, d), BF16)],
        compiler_params=_params("parallel", "arbitrary"),
        name="ffn",
    )(x, g, w_up, w_down)


def _ple_kernel(final, tn, x_ref, g_ref, wg_ref, p_ref, wp_ref, fg_ref, o_ref):
    g = g_ref[...]
    parts = _row_parts(x_ref.shape[0])
    hs = [_rms_rows(x_ref[rs, :], g).astype(BF16) for rs in parts]
    for rs, h in zip(parts, hs):
        pb = p_ref[rs, :].astype(BF16)
        for c in range(x_ref.shape[1] // tn):
            cs = slice(c * tn, (c + 1) * tn)
            gate = jax.nn.sigmoid(_dot(h, wg_ref[:, cs]))
            o_ref[rs, cs] = x_ref[rs, cs] + gate * _dot(pb, wp_ref[:, cs])
    if final:
        _norm_into(o_ref, o_ref, fg_ref)


def _ple(x, g, w_gate, p, w_proj, final_g, layer, final, tm):
    m, d = x.shape
    dp = p.shape[1]
    tm = min(tm, m)
    nblk = m // tm
    return pl.pallas_call(
        functools.partial(_ple_kernel, final, min(512, d)),
        grid=(nblk,),
        in_specs=[
            pl.BlockSpec((tm, d), lambda i: (i, 0)),
            pl.BlockSpec((None, 1, d), lambda i: (layer, 0, 0)),
            _resident((d, d), lambda i: (0, 0)),
            pl.BlockSpec((tm, dp), lambda i: (layer * nblk + i, 0)),
            _resident((None, dp, d), lambda i: (layer, 0, 0)),
            pl.BlockSpec((1, d), lambda i: (0, 0)),
        ],
        out_specs=pl.BlockSpec((tm, d), lambda i: (i, 0)),
        out_shape=jax.ShapeDtypeStruct((m, d), F32),
        compiler_params=_params("parallel"),
        name="ple",
    )(x, g, w_gate, p, w_proj, final_g)


def _pool_kernel(z_ref, halo_ref, pw_ref, ps_ref, o_ref):
    i = pl.program_id(1)
    tt, width = z_ref.shape[1], z_ref.shape[2]
    pg = width // len(POOL_WINDOWS)
    u = z_ref[0]
    halo = jnp.where(i == 0, 0.0, halo_ref[0])
    cur = jnp.concatenate([halo, u], axis=0)
    span, lane0 = 1, 0
    t = i * tt + lax.broadcasted_iota(jnp.int32, (tt, pg), 0)
    for gi, win in enumerate(POOL_WINDOWS):
        cur = cur[:, gi * pg - lane0:]
        lane0 = gi * pg
        while span < win:
            cur = cur + pltpu.roll(cur, span, 0)
            span *= 2
        cols = slice(gi * pg, (gi + 1) * pg)
        cnt = jnp.minimum(t + 1, win).astype(F32)
        d = cur[POOL_HALO:, :pg] / cnt - u[:, cols]
        o = _dot(d.astype(BF16), pw_ref[gi]) * ps_ref[:, cols]
        o_ref[0, :, cols] = o.astype(o_ref.dtype)


def _pool(z, pool_w, pool_scale, layer, width, tt):
    b, t, _ = z.shape
    g, pg, _ = pool_w.shape[1:]
    tt = min(tt, t)
    assert all(w & (w - 1) == 0 for w in POOL_WINDOWS) and max(POOL_WINDOWS) <= POOL_HALO
    assert list(POOL_WINDOWS) == sorted(POOL_WINDOWS) and g == len(POOL_WINDOWS) and g * pg == width
    hb = tt // POOL_HALO
    return pl.pallas_call(
        _pool_kernel,
        grid=(b, t // tt),
        in_specs=[
            pl.BlockSpec((1, tt, width), lambda bi, i: (bi, i, 0)),
            pl.BlockSpec((1, POOL_HALO, width), lambda bi, i: (bi, jnp.maximum(i * hb - 1, 0), 0)),
            pl.BlockSpec((None, g, pg, pg), lambda bi, i: (layer, 0, 0, 0)),
            pl.BlockSpec((None, 1, width), lambda bi, i: (layer, 0, 0)),
        ],
        out_specs=pl.BlockSpec((1, tt, width), lambda bi, i: (bi, i, 0)),
        out_shape=jax.ShapeDtypeStruct((b, t, width), BF16),
        compiler_params=_params("parallel", "parallel"),
        name="pool",
    )(z, z, pool_w, pool_scale)


def _prep_kernel(first, dims, zr_ref, zk_ref, zv_ref, zl_ref, hr_ref, hk_ref, hv_ref, hl_ref,
                 mu_ref, vec_ref, w1_ref, w2_ref, e_ref, tri_ref, vf_ref,
                 at_o, rt_o, bt_o, kt_o, vb_o, wl_o, bv_o, g_o, *v_o):
    width, d_w, d_g = dims
    i = pl.program_id(1)
    tt = zr_ref.shape[1]

    def shifted(z_ref, h_ref, mu):
        z = z_ref[0]
        prev = jnp.where(i == 0, 0.0, h_ref[0, SHIFT_HALO - 1:SHIFT_HALO, :])
        zp = pltpu.roll(z, 1, 0)
        row = lax.broadcasted_iota(jnp.int32, z.shape, 0)
        zp = jnp.where(row == 0, prev, zp)
        return z + (zp - z) * mu

    w0, a0, v0 = vec_ref[0:1, :], vec_ref[1:2, :], vec_ref[2:3, :]
    k_k, k_a, r_k = vec_ref[3:4, :], vec_ref[4:5, :], vec_ref[5:6, :]

    lz = shifted(zl_ref, hl_ref, mu_ref[:, 3 * width:])
    s1 = lz[:, :LANE]
    lane1 = lax.broadcasted_iota(jnp.int32, s1.shape, 1)
    act1 = jnp.where(lane1 < d_w, jnp.tanh(s1), s1)
    lo1 = _dot(act1.astype(BF16), w1_ref[...])
    s2 = lz[:, LANE:LANE + w2_ref.shape[0]]
    lane2 = lax.broadcasted_iota(jnp.int32, s2.shape, 1)
    act2 = jnp.where(lane2 < d_g, jax.nn.sigmoid(s2), s2)
    lo2 = _dot(act2.astype(BF16), w2_ref[...])
    g_o[0] = lo2[:, :width]

    lw = -EXP_NEG_HALF * jax.nn.sigmoid(w0 + lo1[:, :width])
    h1 = lw.astype(BF16)
    r1 = lw - h1.astype(F32)
    h2 = r1.astype(BF16)
    h3 = (r1 - h2.astype(F32)).astype(BF16)
    tri = tri_ref[...]
    sums = _dot(tri, h1) + _dot(tri, h2) + _dot(tri, h3)
    cum, tot = sums[:tt], sums[tt:]
    e_inv = jnp.exp(-cum)
    for c in range(tt // WKV_CHUNK):
        wl_o[0, c] = jnp.exp(tot[c * WKV_CHUNK:c * WKV_CHUNK + 1, :])

    a = jax.nn.sigmoid(a0 + lo1[:, width:])
    r = shifted(zr_ref, hr_ref, mu_ref[:, :width])
    k = shifted(zk_ref, hk_ref, mu_ref[:, width:2 * width])
    v = shifted(zv_ref, hv_ref, mu_ref[:, 2 * width:3 * width])
    if first:
        v_o[0][0] = v
    else:
        v = v + (vf_ref[0] - v) * jax.nn.sigmoid(v0 + lo2[:, width:])
    vb_o[0] = v.astype(BF16)

    x = k * k_k
    kk = x * lax.rsqrt(jnp.maximum(_segsum(x * x, e_ref[...]), 1e-24))
    k2 = k * (1.0 + (a - 1.0) * k_a)
    b = kk * a
    at_o[0] = (-kk * jnp.exp(cum - lw)).astype(BF16)
    rt_o[0] = (r * jnp.exp(cum)).astype(BF16)
    bt_o[0] = (b * e_inv).astype(BF16)
    kt_o[0] = (k2 * e_inv).astype(BF16)
    bv_o[0] = _segsum(r * k2 * r_k, e_ref[...]) * v


def _prep(z, mu, vecs, w1, w2, e, v_first, layer, first, width, d_w, d_g, tt):
    b, t, zw = z.shape
    tt = min(tt, t)
    lw = zw - 4 * width
    hb = tt // SHIFT_HALO
    lcol = 4 * width // lw
    nchunk = tt // WKV_CHUNK
    assert 4 * width % lw == 0 and tt % WKV_CHUNK == 0
    row = jnp.arange(tt)
    same_chunk = (row[:, None] // WKV_CHUNK) == (row[None, :] // WKV_CHUNK)
    tri = jnp.concatenate([same_chunk & (row[None, :] <= row[:, None]), same_chunk], axis=0).astype(BF16)

    def main(col, w):
        return pl.BlockSpec((1, tt, w), lambda bi, i: (bi, i, col))

    def halo(col, w):
        return pl.BlockSpec((1, SHIFT_HALO, w), lambda bi, i: (bi, jnp.maximum(i * hb - 1, 0), col))

    def whole(a):
        return pl.BlockSpec((None,) + a.shape[1:], lambda bi, i: (layer,) + (0,) * (a.ndim - 1))

    tile = pl.BlockSpec((1, tt, width), lambda bi, i: (bi, i, 0))
    full = lambda dt: jax.ShapeDtypeStruct((b, t, width), dt)
    out_specs = [tile] * 5 + [pl.BlockSpec((1, nchunk, 1, width), lambda bi, i: (bi, i, 0, 0)), tile, tile]
    out_shape = [full(BF16)] * 5 + [jax.ShapeDtypeStruct((b, t // WKV_CHUNK, 1, width), F32),
                                    full(F32), full(F32)]
    if first:
        out_specs.append(tile)
        out_shape.append(full(F32))
    return pl.pallas_call(
        functools.partial(_prep_kernel, first, (width, d_w, d_g)),
        grid=(b, t // tt),
        in_specs=[
            main(1, width), main(2, width), main(3, width), main(lcol, lw),
            halo(1, width), halo(2, width), halo(3, width), halo(lcol, lw),
            whole(mu), whole(vecs), whole(w1), whole(w2),
            pl.BlockSpec(e.shape, lambda bi, i: (0, 0)),
            pl.BlockSpec(tri.shape, lambda bi, i: (0, 0)),
            tile,
        ],
        out_specs=out_specs,
        out_shape=out_shape,
        compiler_params=_params("parallel", "parallel"),
        name="rwkv_prep",
    )(z, z, z, z, z, z, z, z, mu, vecs, w1, w2, e, tri, v_first)


def _wkv_kernel(n_cast, n_join, at_ref, rt_ref, bt_ref, kt_ref, v_ref, wl_ref, *refs):
    n_in = n_cast + n_join
    w_refs, join_refs, y_ref, s_ref = refs[:n_cast], refs[n_cast:n_in], refs[n_in], refs[-1]
    w_outs = refs[n_in + 1:-1]
    hg = WKV_HEADS_PER_GROUP
    c = pl.program_id(1)
    nrow, L, C = at_ref.shape
    W, GL = hg * HEAD_SIZE, hg * L
    ng = C // W

    for w_ref, w_out in zip(w_refs, w_outs):
        w_out[...] = w_ref[...].astype(BF16)
    if join_refs:
        joined = w_outs[n_cast]
        lead = join_refs[0].shape[1] // LANE * LANE
        joined[:, :lead] = join_refs[0][:, :lead].astype(BF16)
        used = sum(r.shape[1] for r in join_refs)
        tail = [join_refs[0][:, lead:]] + [r[...] for r in join_refs[1:]]
        tail.append(jnp.zeros((joined.shape[0], joined.shape[1] - used), F32))
        joined[:, lead:] = jnp.concatenate(tail, axis=1).astype(BF16)

    @pl.when(c == 0)
    def _():
        s_ref[...] = jnp.zeros_like(s_ref)

    lane = lax.broadcasted_iota(jnp.int32, (L, W), 1)
    head_mask = [(lane >= h * HEAD_SIZE) & (lane < (h + 1) * HEAD_SIZE) for h in range(hg)]
    row2 = lax.broadcasted_iota(jnp.int32, (L, 2 * GL), 0)
    col2 = lax.broadcasted_iota(jnp.int32, (L, 2 * GL), 1) & (L - 1)
    strict = col2 < row2
    incl = col2 <= row2
    colg = lax.broadcasted_iota(jnp.int32, (L, GL), 1)
    rowg = lax.broadcasted_iota(jnp.int32, (L, GL), 0)
    eye = (colg & (L - 1)) == rowg
    blk_mask = [(colg >= h * L) & (colg < (h + 1) * L) for h in range(hg)]
    sr = lax.broadcasted_iota(jnp.int32, (W, W), 0)
    sc = lax.broadcasted_iota(jnp.int32, (W, W), 1)
    same_head = (sr // HEAD_SIZE) == (sc // HEAD_SIZE)

    def stack_heads(x):
        return jnp.concatenate([jnp.where(m, x, 0.0).astype(BF16) for m in head_mask], axis=0)

    def block_diag(q):
        return jnp.concatenate([jnp.where(m, q, 0.0).astype(BF16) for m in blk_mask], axis=0)

    chains = [(b, g) for b in range(nrow) for g in range(ng)]
    ids = range(len(chains))

    def op(ref, i):
        b, g = chains[i]
        return ref[b, :, g * W:(g + 1) * W]

    lhs = [jnp.concatenate([op(at_ref, i), op(rt_ref, i)], axis=0) for i in ids]
    gm = [_dot_nt(lhs[i], jnp.concatenate([stack_heads(op(bt_ref, i)), stack_heads(op(kt_ref, i))],
                                          axis=0)) for i in ids]
    s0 = [s_ref[i] for i in ids]
    as0 = [_dot_nt(lhs[i], s0[i].astype(BF16)) for i in ids]
    pa = [jnp.where(strict, gm[i][:L], 0.0) for i in ids]
    pr = [jnp.where(incl, gm[i][L:], 0.0) for i in ids]
    vm = [stack_heads(op(v_ref, i)) for i in ids]

    tm = [jnp.where(eye, 1.0, pa[i][:, :GL]) for i in ids]
    q = [_dot(pa[i][:, :GL].astype(BF16), block_diag(pa[i][:, :GL])) for i in ids]
    pkv = [_dot(jnp.concatenate([pa[i][:, GL:].astype(BF16), pr[i][:, GL:].astype(BF16)], axis=0),
                vm[i]) for i in ids]
    x1 = [as0[i][:L] + pkv[i][:L] for i in ids]
    levels = L.bit_length() - 2
    for j in range(levels):
        if j + 1 < levels:
            res = [_dot(jnp.concatenate([q[i].astype(BF16), tm[i].astype(BF16)], axis=0),
                        block_diag(q[i])) for i in ids]
            q = [res[i][:L] for i in ids]
            tm = [tm[i] + res[i][L:] for i in ids]
        else:
            tm = [tm[i] + _dot(tm[i].astype(BF16), block_diag(q[i])) for i in ids]

    u = [_dot(tm[i].astype(BF16), stack_heads(x1[i])) for i in ids]
    for i, (b, g) in enumerate(chains):
        y_ref[b, :, g * W:(g + 1) * W] = (as0[i][L:] + pkv[i][L:]
                                          + _dot(pr[i][:, :GL].astype(BF16), stack_heads(u[i])))
    for i, (b, g) in enumerate(chains):
        uv = jnp.concatenate([u[i].astype(BF16), op(v_ref, i)], axis=0)
        bk = jnp.concatenate([op(bt_ref, i), op(kt_ref, i)], axis=0)
        ds = _dot_tn(uv, bk)
        wl = wl_ref[b, 0, :, g * W:(g + 1) * W]
        s_ref[i] = s0[i] * wl + jnp.where(same_head, ds, 0.0) * wl


def _wkv(at, rt, bt, kt, vb, wl, weights, layer, join=(), join_layer=0, join_width=0):
    bsz, t, c = at.shape
    L = WKV_CHUNK
    w = WKV_HEADS_PER_GROUP * HEAD_SIZE
    assert L & (L - 1) == 0 and t % L == 0 and c % w == 0
    nrow = WKV_ROWS_PER_STEP if bsz % WKV_ROWS_PER_STEP == 0 else 1
    steps = t // L
    slabs = (bsz // nrow) * steps
    assert all(wt.shape[1] % slabs == 0 for wt in tuple(weights) + tuple(join))
    spec = pl.BlockSpec((nrow, L, c), lambda bi, ci: (bi, ci, 0))
    w_in = [pl.BlockSpec((None, wt.shape[1] // slabs, wt.shape[2]),
                         lambda bi, ci: (layer, bi * steps + ci, 0)) for wt in weights]
    w_out = [pl.BlockSpec((wt.shape[1] // slabs, wt.shape[2]),
                          lambda bi, ci: (bi * steps + ci, 0)) for wt in weights]
    w_shape = [jax.ShapeDtypeStruct(wt.shape[1:], BF16) for wt in weights]
    if join:
        rows = join[0].shape[1]
        w_in += [pl.BlockSpec((None, rows // slabs, wt.shape[2]),
                              lambda bi, ci: (join_layer, bi * steps + ci, 0)) for wt in join]
        w_out.append(pl.BlockSpec((rows // slabs, join_width), lambda bi, ci: (bi * steps + ci, 0)))
        w_shape.append(jax.ShapeDtypeStruct((rows, join_width), BF16))
    y, *cast = pl.pallas_call(
        functools.partial(_wkv_kernel, len(weights), len(join)),
        grid=(bsz // nrow, steps),
        in_specs=[spec] * 5 + [pl.BlockSpec((nrow, 1, 1, c), lambda bi, ci: (bi, ci, 0, 0))] + w_in,
        out_specs=[spec] + w_out,
        out_shape=[jax.ShapeDtypeStruct((bsz, t, c), F32)] + w_shape,
        scratch_shapes=[pltpu.VMEM((nrow * (c // w), w, w), F32)],
        compiler_params=_params("parallel", "arbitrary"),
        name="wkv_scan",
    )(at, rt, bt, kt, vb, wl, *weights, *join)
    return y, cast


def _post_kernel(y_ref, bv_ref, g_ref, vec_ref, e_ref, o_ref):
    e = e_ref[...]
    gn_g, gn_b = vec_ref[0:1, :], vec_ref[1:2, :]
    inv_n = 1.0 / HEAD_SIZE
    y = y_ref[0]
    d = y - _segsum(y, e) * inv_n
    var = _segsum(d * d, e) * inv_n
    yn = d * lax.rsqrt(var + GN_EPS) * gn_g + gn_b
    o_ref[0] = ((yn + bv_ref[0]) * g_ref[0]).astype(o_ref.dtype)


def _post(y, bv, g, vecs, e, layer, tt):
    b, t, c = y.shape
    tt = min(tt, t)
    spec = pl.BlockSpec((1, tt, c), lambda bi, i: (bi, i, 0))
    return pl.pallas_call(
        _post_kernel,
        grid=(b, t // tt),
        in_specs=[spec] * 3 + [
            pl.BlockSpec((None,) + vecs.shape[1:], lambda bi, i: (layer, 0, 0)),
            pl.BlockSpec(e.shape, lambda bi, i: (0, 0)),
        ],
        out_specs=spec,
        out_shape=jax.ShapeDtypeStruct((b, t, c), BF16),
        compiler_params=_params("parallel", "parallel"),
        name="rwkv_post",
    )(y, bv, g, vecs, e)


def kernel(x, p, attn_norm, w_in, mu_shift, w_vres_dn, mu_vres, v0, v_up, pool_w, pool_scale,
           w0, w_up, a0, a_up, g_up, k_k, k_a, r_k, gn_g, gn_b, w_out, mlp_norm, w_ffn_up,
           w_ffn_down, ple_norm, w_ple_gate, w_ple_proj, final_norm):
    bsz, t, d = x.shape
    depth = w_in.shape[0]
    width = w0.shape[1]
    pool_width = pool_scale.shape[1]
    d_w, d_a, d_g, d_v = w_up.shape[1], a_up.shape[1], g_up.shape[1], v_up.shape[1]
    d_ple = p.shape[-1]
    m = bsz * t
    assert pool_width == width and d_w + d_a == LANE and width % SEG_SLAB == 0
    lora_w = 2 * SEG_SLAB
    w2_rows = lora_w - LANE - LANE
    assert d_g + d_v <= w2_rows
    zw = pool_width + 3 * width + lora_w

    n_in = w_in.shape[2]
    vres = jnp.concatenate([jnp.zeros((1, d, d_v), F32), w_vres_dn], axis=0)
    w_in_b = jnp.concatenate(
        [w_in[0].astype(BF16), jnp.zeros((d, zw - n_in), BF16)], axis=1)
    mu_vres_x = jnp.concatenate([jnp.zeros((1, d_v), F32), mu_vres], axis=0)
    mu_x = jnp.concatenate(
        [mu_shift, mu_vres_x, jnp.zeros((depth, zw - n_in - d_v), F32)], axis=1)[:, None, :]
    zeros_w = lambda rows: jnp.zeros((depth, rows, width), F32)
    w1 = jnp.concatenate([
        jnp.concatenate([w_up, zeros_w(d_w)], axis=2),
        jnp.concatenate([zeros_w(d_a), a_up], axis=2)], axis=1).astype(BF16)
    v_up_x = jnp.concatenate([jnp.zeros((1, d_v, width), F32), v_up], axis=0)
    w2 = jnp.concatenate([
        jnp.concatenate([g_up, zeros_w(d_g)], axis=2),
        jnp.concatenate([zeros_w(d_v), v_up_x], axis=2),
        jnp.zeros((depth, w2_rows - d_g - d_v, 2 * width), F32)], axis=1).astype(BF16)
    v0_x = jnp.concatenate([jnp.zeros((1, width), F32), v0], axis=0)
    pad = lambda rows: jnp.zeros((depth, rows, width), F32)
    prep_vecs = jnp.concatenate(
        [jnp.stack([w0, a0, v0_x, k_k, k_a, r_k.reshape(depth, width)], axis=1), pad(2)],
        axis=1)
    post_vecs = jnp.concatenate([jnp.stack([gn_g, gn_b], axis=1), pad(6)], axis=1)
    seg = jnp.arange(SEG_SLAB) // HEAD_SIZE
    e = (seg[:, None] == seg[None, :]).astype(BF16)
    attn_g, mlp_g, ple_g = attn_norm[:, None, :], mlp_norm[:, None, :], ple_norm[:, None, :]
    pool_wb = pool_w.astype(BF16)
    pool_sc = pool_scale[:, None, :]
    w_proj_b = w_ple_proj.astype(BF16)
    p2 = p.reshape(depth * m, d_ple)
    final_g = final_norm[None, :]

    xf = x.reshape(m, d)
    v_first = None
    for i in range(depth):
        z = _rms_matmul(xf, attn_g, w_in_b, i, tm=512, tn=768).reshape(bsz, t, zw)
        pool_out = _pool(z, pool_wb, pool_sc, i, pool_width, tt=512)
        first = i == 0
        at, rt, bt, kt, vb, wl, bv, g, *v_new = _prep(
            z, mu_x, prep_vecs, w1, w2, e, z if first else v_first, i, first, width, d_w, d_g, tt=256)
        if first:
            v_first = v_new[0]
        last = i == depth - 1
        y, cast = _wkv(at, rt, bt, kt, vb, wl, (w_out, w_ffn_up, w_ffn_down, w_ple_gate), i,
                       join=() if last else (w_in, vres), join_layer=i + 1, join_width=zw)
        w_out_b, w_up_b, w_down_b, w_gate_b = cast[:4]
        if not last:
            w_in_b = cast[4]
        rwkv_out = _post(y, bv, g, post_vecs, e, i, tt=512)
        xf = _outproj(xf, pool_out.reshape(m, pool_width), rwkv_out.reshape(m, width), w_out_b,
                      tm=512, tn=512)
        xf = _ffn(xf, mlp_g, w_up_b, w_down_b, i, tm=512, tf=1024)
        xf = _ple(xf, ple_g, w_gate_b, p2, w_proj_b, final_g, i, i == depth - 1, tm=512)
    return xf.reshape(bsz, t, d)
```

```python
import functools

import jax
import jax.numpy as jnp
from jax import lax
from jax.experimental import pallas as pl
from jax.experimental.pallas import tpu as pltpu

F32 = jnp.float32
BF16 = jnp.bfloat16

NORM_EPS = 1e-6
GN_EPS = 64e-5
EXP_NEG_HALF = 0.6065306597126334
HEAD_SIZE = 64
POOL_WINDOWS = (2, 4, 8, 16)
POOL_HALO = 16
SHIFT_HALO = 8
WKV_CHUNK = 64
WKV_HEADS_PER_GROUP = 2
WKV_ROWS_PER_STEP = 4
SEG_SLAB = 256
LANE = 128
VMEM_LIMIT = 52 * 1024 * 1024


def _dot(a, b):
    return jnp.dot(a, b, preferred_element_type=F32)


def _dot_nt(a, b):
    return lax.dot_general(a, b, (((1,), (1,)), ((), ())), preferred_element_type=F32)


def _dot_tn(a, b):
    return lax.dot_general(a, b, (((0,), (0,)), ((), ())), preferred_element_type=F32)


def _params(*sem):
    return pltpu.CompilerParams(dimension_semantics=sem, vmem_limit_bytes=VMEM_LIMIT)


def _rms_rows(x, g):
    ms = jnp.mean(x * x, axis=-1, keepdims=True)
    return x * lax.rsqrt(ms + NORM_EPS) * g


def _norm_into(h_ref, x_ref, g_ref, rows=256):
    rows = min(rows, x_ref.shape[0])
    n = x_ref.shape[0] // rows

    def body(c, carry):
        rs = pl.ds(pl.multiple_of(c * rows, rows), rows)
        h_ref[rs, :] = _rms_rows(x_ref[rs, :], g_ref[...]).astype(h_ref.dtype)
        return carry

    lax.fori_loop(0, n, body, 0)


def _segsum(x, e):
    outs = []
    for s in range(x.shape[1] // SEG_SLAB):
        xs = x[:, s * SEG_SLAB:(s + 1) * SEG_SLAB]
        hi = xs.astype(BF16)
        lo = (xs - hi.astype(F32)).astype(BF16)
        outs.append(_dot(hi, e) + _dot(lo, e))
    return jnp.concatenate(outs, axis=1)


def _row_parts(n_rows, parts=2):
    step = n_rows // parts
    return [slice(s * step, (s + 1) * step) for s in range(parts)]


def _rms_matmul_kernel(tn, x_ref, g_ref, w_ref, o_ref):
    g = g_ref[...]
    parts = _row_parts(x_ref.shape[0])
    hs = [_rms_rows(x_ref[rs, :], g).astype(BF16) for rs in parts]
    for rs, h in zip(parts, hs):
        for c in range(w_ref.shape[1] // tn):
            cs = slice(c * tn, (c + 1) * tn)
            o_ref[rs, cs] = _dot(h, w_ref[:, cs])


def _resident(block_shape, index_map):
    return pl.BlockSpec(block_shape, index_map, pipeline_mode=pl.Buffered(1))


def _rms_matmul(x, g, w, layer, tm, tn):
    m, d = x.shape
    n = w.shape[1]
    tm, tn = min(tm, m), min(tn, n)
    return pl.pallas_call(
        functools.partial(_rms_matmul_kernel, tn),
        grid=(m // tm,),
        in_specs=[
            pl.BlockSpec((tm, d), lambda i: (i, 0)),
            pl.BlockSpec((None, 1, d), lambda i: (layer, 0, 0)),
            _resident((d, n), lambda i: (0, 0)),
        ],
        out_specs=pl.BlockSpec((tm, n), lambda i: (i, 0)),
        out_shape=jax.ShapeDtypeStruct((m, n), F32),
        compiler_params=_params("parallel"),
        name="rms_matmul",
    )(x, g, w)


def _pool_windows(tile, u, halo):
    tt, width = u.shape
    pg = width // len(POOL_WINDOWS)
    cur = jnp.concatenate([halo, u], axis=0)
    span, lane0 = 1, 0
    t = tile * tt + lax.broadcasted_iota(jnp.int32, (tt, pg), 0)
    outs = []
    for gi, win in enumerate(POOL_WINDOWS):
        cur = cur[:, gi * pg - lane0:]
        lane0 = gi * pg
        while span < win:
            cur = cur + pltpu.roll(cur, span, 0)
            span *= 2
        cnt = jnp.minimum(t + 1, win).astype(F32)
        outs.append((cur[POOL_HALO:, :pg] / cnt - u[:, gi * pg:(gi + 1) * pg]).astype(BF16))
    return outs


def _outproj_kernel(tn, tiles_per_seq, x_ref, z_ref, halo_ref, mb_ref, pw_ref, ps_ref, w_ref, o_ref):
    tile = pl.program_id(0) % tiles_per_seq
    ka = z_ref.shape[1]
    pg = ka // len(POOL_WINDOWS)
    chunks = [slice(c * tn, (c + 1) * tn) for c in range(x_ref.shape[1] // tn)]
    halo = jnp.where(tile == 0, 0.0, halo_ref[...])
    pooled = _pool_windows(tile, z_ref[...], halo)
    for cs in chunks:
        o_ref[:, cs] = x_ref[:, cs] + _dot(mb_ref[...], w_ref[ka:, cs])
    pool_out = jnp.concatenate(
        [(_dot(d, pw_ref[gi]) * ps_ref[:, gi * pg:(gi + 1) * pg]).astype(BF16)
         for gi, d in enumerate(pooled)], axis=1)
    for cs in chunks:
        o_ref[:, cs] += _dot(pool_out, w_ref[:ka, cs])


def _outproj(x, z, rwkv_out, pool_w, pool_scale, w, layer, seq, tm, tn):
    m, d = x.shape
    g, pg, _ = pool_w.shape[1:]
    ka, kb = g * pg, rwkv_out.shape[1]
    tm, tn = min(tm, seq), min(tn, d)
    assert all(w_ & (w_ - 1) == 0 for w_ in POOL_WINDOWS) and max(POOL_WINDOWS) <= POOL_HALO
    assert list(POOL_WINDOWS) == sorted(POOL_WINDOWS) and g == len(POOL_WINDOWS) and seq % tm == 0
    hb = tm // POOL_HALO
    return pl.pallas_call(
        functools.partial(_outproj_kernel, tn, seq // tm),
        grid=(m // tm,),
        in_specs=[
            pl.BlockSpec((tm, d), lambda i: (i, 0)),
            pl.BlockSpec((tm, ka), lambda i: (i, 0)),
            pl.BlockSpec((POOL_HALO, ka), lambda i: (jnp.maximum(i * hb - 1, 0), 0)),
            pl.BlockSpec((tm, kb), lambda i: (i, 0)),
            pl.BlockSpec((None, g, pg, pg), lambda i: (layer, 0, 0, 0)),
            pl.BlockSpec((None, 1, ka), lambda i: (layer, 0, 0)),
            _resident((ka + kb, d), lambda i: (0, 0)),
        ],
        out_specs=pl.BlockSpec((tm, d), lambda i: (i, 0)),
        out_shape=jax.ShapeDtypeStruct((m, d), F32),
        compiler_params=_params("parallel"),
        name="outproj",
    )(x, z, z, rwkv_out, pool_w, pool_scale, w)


def _ffn_kernel(x_ref, g_ref, wu_ref, wd_ref, o_ref, h_ref):
    @pl.when(pl.program_id(1) == 0)
    def _():
        _norm_into(h_ref, x_ref, g_ref)
        o_ref[...] = x_ref[...]

    a = _dot(h_ref[...], wu_ref[...])
    a = jnp.square(jnp.maximum(a, 0.0)).astype(BF16)
    o_ref[...] += _dot(a, wd_ref[...])


def _ffn(x, g, w_up, w_down, layer, tm, tf):
    m, d = x.shape
    f = w_up.shape[1]
    tm, tf = min(tm, m), min(tf, f)
    return pl.pallas_call(
        _ffn_kernel,
        grid=(m // tm, f // tf),
        in_specs=[
            pl.BlockSpec((tm, d), lambda i, j: (i, 0)),
            pl.BlockSpec((None, 1, d), lambda i, j: (layer, 0, 0)),
            pl.BlockSpec((d, tf), lambda i, j: (0, j)),
            pl.BlockSpec((tf, d), lambda i, j: (j, 0)),
        ],
        out_specs=pl.BlockSpec((tm, d), lambda i, j: (i, 0)),
        out_shape=jax.ShapeDtypeStruct((m, d), F32),
        scratch_shapes=[pltpu.VMEM((tm, d), BF16)],
        compiler_params=_params("parallel", "arbitrary"),
        name="ffn",
    )(x, g, w_up, w_down)


def _ple_kernel(final, tn, x_ref, g_ref, wg_ref, p_ref, wp_ref, fg_ref, o_ref):
    g = g_ref[...]
    parts = _row_parts(x_ref.shape[0])
    hs = [_rms_rows(x_ref[rs, :], g).astype(BF16) for rs in parts]
    for rs, h in zip(parts, hs):
        pb = p_ref[rs, :].astype(BF16)
        for c in range(x_ref.shape[1] // tn):
            cs = slice(c * tn, (c + 1) * tn)
            gate = jax.nn.sigmoid(_dot(h, wg_ref[:, cs]))
            o_ref[rs, cs] = x_ref[rs, cs] + gate * _dot(pb, wp_ref[:, cs])
    if final:
        _norm_into(o_ref, o_ref, fg_ref)


def _ple(x, g, w_gate, p, w_proj, final_g, layer, final, tm):
    m, d = x.shape
    dp = p.shape[1]
    tm = min(tm, m)
    nblk = m // tm
    return pl.pallas_call(
        functools.partial(_ple_kernel, final, min(512, d)),
        grid=(nblk,),
        in_specs=[
            pl.BlockSpec((tm, d), lambda i: (i, 0)),
            pl.BlockSpec((None, 1, d), lambda i: (layer, 0, 0)),
            _resident((d, d), lambda i: (0, 0)),
            pl.BlockSpec((tm, dp), lambda i: (layer * nblk + i, 0)),
            _resident((None, dp, d), lambda i: (layer, 0, 0)),
            pl.BlockSpec((1, d), lambda i: (0, 0)),
        ],
        out_specs=pl.BlockSpec((tm, d), lambda i: (i, 0)),
        out_shape=jax.ShapeDtypeStruct((m, d), F32),
        compiler_params=_params("parallel"),
        name="ple",
    )(x, g, w_gate, p, w_proj, final_g)


def _prep_kernel(first, dims, zr_ref, zk_ref, zv_ref, zl_ref, hr_ref, hk_ref, hv_ref, hl_ref,
                 mu_ref, vec_ref, w1_ref, w2_ref, e_ref, tri_ref, vf_ref,
                 at_o, rt_o, bt_o, kt_o, vb_o, wl_o, bv_o, g_o, *v_o):
    width, d_w, d_g = dims
    i = pl.program_id(1)
    tt = zr_ref.shape[1]

    def shifted(z_ref, h_ref, mu):
        z = z_ref[0]
        prev = jnp.where(i == 0, 0.0, h_ref[0, SHIFT_HALO - 1:SHIFT_HALO, :])
        zp = pltpu.roll(z, 1, 0)
        row = lax.broadcasted_iota(jnp.int32, z.shape, 0)
        zp = jnp.where(row == 0, prev, zp)
        return z + (zp - z) * mu

    w0, a0, v0 = vec_ref[0:1, :], vec_ref[1:2, :], vec_ref[2:3, :]
    k_k, k_a, r_k = vec_ref[3:4, :], vec_ref[4:5, :], vec_ref[5:6, :]

    lz = shifted(zl_ref, hl_ref, mu_ref[:, 3 * width:])
    s1 = lz[:, :LANE]
    lane1 = lax.broadcasted_iota(jnp.int32, s1.shape, 1)
    act1 = jnp.where(lane1 < d_w, jnp.tanh(s1), s1)
    lo1 = _dot(act1.astype(BF16), w1_ref[...])
    s2 = lz[:, LANE:LANE + w2_ref.shape[0]]
    lane2 = lax.broadcasted_iota(jnp.int32, s2.shape, 1)
    act2 = jnp.where(lane2 < d_g, jax.nn.sigmoid(s2), s2)
    lo2 = _dot(act2.astype(BF16), w2_ref[...])
    g_o[0] = lo2[:, :width]

    lw = -EXP_NEG_HALF * jax.nn.sigmoid(w0 + lo1[:, :width])
    h1 = lw.astype(BF16)
    r1 = lw - h1.astype(F32)
    h2 = r1.astype(BF16)
    h3 = (r1 - h2.astype(F32)).astype(BF16)
    tri = tri_ref[...]
    sums = _dot(tri, h1) + _dot(tri, h2) + _dot(tri, h3)
    cum, tot = sums[:tt], sums[tt:]
    e_inv = jnp.exp(-cum)
    for c in range(tt // WKV_CHUNK):
        wl_o[0, c] = jnp.exp(tot[c * WKV_CHUNK:c * WKV_CHUNK + 1, :])

    a = jax.nn.sigmoid(a0 + lo1[:, width:])
    r = shifted(zr_ref, hr_ref, mu_ref[:, :width])
    k = shifted(zk_ref, hk_ref, mu_ref[:, width:2 * width])
    v = shifted(zv_ref, hv_ref, mu_ref[:, 2 * width:3 * width])
    if first:
        v_o[0][0] = v
    else:
        v = v + (vf_ref[0] - v) * jax.nn.sigmoid(v0 + lo2[:, width:])
    vb_o[0] = v.astype(BF16)

    x = k * k_k
    kk = x * lax.rsqrt(jnp.maximum(_segsum(x * x, e_ref[...]), 1e-24))
    k2 = k * (1.0 + (a - 1.0) * k_a)
    b = kk * a
    at_o[0] = (-kk * jnp.exp(cum - lw)).astype(BF16)
    rt_o[0] = (r * jnp.exp(cum)).astype(BF16)
    bt_o[0] = (b * e_inv).astype(BF16)
    kt_o[0] = (k2 * e_inv).astype(BF16)
    bv_o[0] = _segsum(r * k2 * r_k, e_ref[...]) * v


def _prep(z, mu, vecs, w1, w2, e, v_first, layer, first, width, d_w, d_g, tt):
    b, t, zw = z.shape
    tt = min(tt, t)
    lw = zw - 4 * width
    hb = tt // SHIFT_HALO
    lcol = 4 * width // lw
    nchunk = tt // WKV_CHUNK
    assert 4 * width % lw == 0 and tt % WKV_CHUNK == 0
    row = jnp.arange(tt)
    same_chunk = (row[:, None] // WKV_CHUNK) == (row[None, :] // WKV_CHUNK)
    tri = jnp.concatenate([same_chunk & (row[None, :] <= row[:, None]), same_chunk], axis=0).astype(BF16)

    def main(col, w):
        return pl.BlockSpec((1, tt, w), lambda bi, i: (bi, i, col))

    def halo(col, w):
        return pl.BlockSpec((1, SHIFT_HALO, w), lambda bi, i: (bi, jnp.maximum(i * hb - 1, 0), col))

    def whole(a):
        return pl.BlockSpec((None,) + a.shape[1:], lambda bi, i: (layer,) + (0,) * (a.ndim - 1))

    tile = pl.BlockSpec((1, tt, width), lambda bi, i: (bi, i, 0))
    full = lambda dt: jax.ShapeDtypeStruct((b, t, width), dt)
    out_specs = [tile] * 5 + [pl.BlockSpec((1, nchunk, 1, width), lambda bi, i: (bi, i, 0, 0)), tile, tile]
    out_shape = [full(BF16)] * 5 + [jax.ShapeDtypeStruct((b, t // WKV_CHUNK, 1, width), F32),
                                    full(F32), full(F32)]
    if first:
        out_specs.append(tile)
        out_shape.append(full(F32))
    return pl.pallas_call(
        functools.partial(_prep_kernel, first, (width, d_w, d_g)),
        grid=(b, t // tt),
        in_specs=[
            main(1, width), main(2, width), main(3, width), main(lcol, lw),
            halo(1, width), halo(2, width), halo(3, width), halo(lcol, lw),
            whole(mu), whole(vecs), whole(w1), whole(w2),
            pl.BlockSpec(e.shape, lambda bi, i: (0, 0)),
            pl.BlockSpec(tri.shape, lambda bi, i: (0, 0)),
            tile,
        ],
        out_specs=out_specs,
        out_shape=out_shape,
        compiler_params=_params("parallel", "parallel"),
        name="rwkv_prep",
    )(z, z, z, z, z, z, z, z, mu, vecs, w1, w2, e, tri, v_first)


def _wkv_kernel(n_cast, n_join, at_ref, rt_ref, bt_ref, kt_ref, v_ref, wl_ref, *refs):
    n_in = n_cast + n_join
    w_refs, join_refs, y_ref, s_ref = refs[:n_cast], refs[n_cast:n_in], refs[n_in], refs[-1]
    w_outs = refs[n_in + 1:-1]
    hg = WKV_HEADS_PER_GROUP
    c = pl.program_id(1)
    nrow, L, C = at_ref.shape
    W, GL = hg * HEAD_SIZE, hg * L
    ng = C // W

    for w_ref, w_out in zip(w_refs, w_outs):
        w_out[...] = w_ref[...].astype(BF16)
    if join_refs:
        joined = w_outs[n_cast]
        lead = join_refs[0].shape[1] // LANE * LANE
        joined[:, :lead] = join_refs[0][:, :lead].astype(BF16)
        used = sum(r.shape[1] for r in join_refs)
        tail = [join_refs[0][:, lead:]] + [r[...] for r in join_refs[1:]]
        tail.append(jnp.zeros((joined.shape[0], joined.shape[1] - used), F32))
        joined[:, lead:] = jnp.concatenate(tail, axis=1).astype(BF16)

    @pl.when(c == 0)
    def _():
        s_ref[...] = jnp.zeros_like(s_ref)

    lane = lax.broadcasted_iota(jnp.int32, (L, W), 1)
    head_mask = [(lane >= h * HEAD_SIZE) & (lane < (h + 1) * HEAD_SIZE) for h in range(hg)]
    row2 = lax.broadcasted_iota(jnp.int32, (L, 2 * GL), 0)
    col2 = lax.broadcasted_iota(jnp.int32, (L, 2 * GL), 1) & (L - 1)
    strict = col2 < row2
    incl = col2 <= row2
    colg = lax.broadcasted_iota(jnp.int32, (L, GL), 1)
    rowg = lax.broadcasted_iota(jnp.int32, (L, GL), 0)
    eye = (colg & (L - 1)) == rowg
    blk_mask = [(colg >= h * L) & (colg < (h + 1) * L) for h in range(hg)]
    sr = lax.broadcasted_iota(jnp.int32, (W, W), 0)
    sc = lax.broadcasted_iota(jnp.int32, (W, W), 1)
    same_head = (sr // HEAD_SIZE) == (sc // HEAD_SIZE)

    def stack_heads(x):
        return jnp.concatenate([jnp.where(m, x, 0.0).astype(BF16) for m in head_mask], axis=0)

    def block_diag(q):
        return jnp.concatenate([jnp.where(m, q, 0.0).astype(BF16) for m in blk_mask], axis=0)

    chains = [(b, g) for b in range(nrow) for g in range(ng)]
    ids = range(len(chains))

    def op(ref, i):
        b, g = chains[i]
        return ref[b, :, g * W:(g + 1) * W]

    lhs = [jnp.concatenate([op(at_ref, i), op(rt_ref, i)], axis=0) for i in ids]
    gm = [_dot_nt(lhs[i], jnp.concatenate([stack_heads(op(bt_ref, i)), stack_heads(op(kt_ref, i))],
                                          axis=0)) for i in ids]
    s0 = [s_ref[i] for i in ids]
    as0 = [_dot_nt(lhs[i], s0[i].astype(BF16)) for i in ids]
    pa = [jnp.where(strict, gm[i][:L], 0.0) for i in ids]
    pr = [jnp.where(incl, gm[i][L:], 0.0) for i in ids]
    vm = [stack_heads(op(v_ref, i)) for i in ids]

    tm = [jnp.where(eye, 1.0, pa[i][:, :GL]) for i in ids]
    q = [_dot(pa[i][:, :GL].astype(BF16), block_diag(pa[i][:, :GL])) for i in ids]
    pkv = [_dot(jnp.concatenate([pa[i][:, GL:].astype(BF16), pr[i][:, GL:].astype(BF16)], axis=0),
                vm[i]) for i in ids]
    x1 = [as0[i][:L] + pkv[i][:L] for i in ids]
    levels = L.bit_length() - 2
    for j in range(levels):
        if j + 1 < levels:
            res = [_dot(jnp.concatenate([q[i].astype(BF16), tm[i].astype(BF16)], axis=0),
                        block_diag(q[i])) for i in ids]
            q = [res[i][:L] for i in ids]
            tm = [tm[i] + res[i][L:] for i in ids]
        else:
            tm = [tm[i] + _dot(tm[i].astype(BF16), block_diag(q[i])) for i in ids]

    u = [_dot(tm[i].astype(BF16), stack_heads(x1[i])) for i in ids]
    for i, (b, g) in enumerate(chains):
        y_ref[b, :, g * W:(g + 1) * W] = (as0[i][L:] + pkv[i][L:]
                                          + _dot(pr[i][:, :GL].astype(BF16), stack_heads(u[i])))
    for i, (b, g) in enumerate(chains):
        uv = jnp.concatenate([u[i].astype(BF16), op(v_ref, i)], axis=0)
        bk = jnp.concatenate([op(bt_ref, i), op(kt_ref, i)], axis=0)
        ds = _dot_tn(uv, bk)
        wl = wl_ref[b, 0, :, g * W:(g + 1) * W]
        s_ref[i] = s0[i] * wl + jnp.where(same_head, ds, 0.0) * wl


def _wkv(at, rt, bt, kt, vb, wl, weights, layer, join=(), join_layer=0, join_width=0):
    bsz, t, c = at.shape
    L = WKV_CHUNK
    w = WKV_HEADS_PER_GROUP * HEAD_SIZE
    assert L & (L - 1) == 0 and t % L == 0 and c % w == 0
    nrow = WKV_ROWS_PER_STEP if bsz % WKV_ROWS_PER_STEP == 0 else 1
    steps = t // L
    slabs = (bsz // nrow) * steps
    assert all(wt.shape[1] % slabs == 0 for wt in tuple(weights) + tuple(join))
    spec = pl.BlockSpec((nrow, L, c), lambda bi, ci: (bi, ci, 0))
    w_in = [pl.BlockSpec((None, wt.shape[1] // slabs, wt.shape[2]),
                         lambda bi, ci: (layer, bi * steps + ci, 0)) for wt in weights]
    w_out = [pl.BlockSpec((wt.shape[1] // slabs, wt.shape[2]),
                          lambda bi, ci: (bi * steps + ci, 0)) for wt in weights]
    w_shape = [jax.ShapeDtypeStruct(wt.shape[1:], BF16) for wt in weights]
    if join:
        rows = join[0].shape[1]
        w_in += [pl.BlockSpec((None, rows // slabs, wt.shape[2]),
                              lambda bi, ci: (join_layer, bi * steps + ci, 0)) for wt in join]
        w_out.append(pl.BlockSpec((rows // slabs, join_width), lambda bi, ci: (bi * steps + ci, 0)))
        w_shape.append(jax.ShapeDtypeStruct((rows, join_width), BF16))
    y, *cast = pl.pallas_call(
        functools.partial(_wkv_kernel, len(weights), len(join)),
        grid=(bsz // nrow, steps),
        in_specs=[spec] * 5 + [pl.BlockSpec((nrow, 1, 1, c), lambda bi, ci: (bi, ci, 0, 0))] + w_in,
        out_specs=[spec] + w_out,
        out_shape=[jax.ShapeDtypeStruct((bsz, t, c), F32)] + w_shape,
        scratch_shapes=[pltpu.VMEM((nrow * (c // w), w, w), F32)],
        compiler_params=_params("parallel", "arbitrary"),
        name="wkv_scan",
    )(at, rt, bt, kt, vb, wl, *weights, *join)
    return y, cast


def _post_kernel(y_ref, bv_ref, g_ref, vec_ref, e_ref, o_ref):
    e = e_ref[...]
    gn_g, gn_b = vec_ref[0:1, :], vec_ref[1:2, :]
    inv_n = 1.0 / HEAD_SIZE
    y = y_ref[0]
    d = y - _segsum(y, e) * inv_n
    var = _segsum(d * d, e) * inv_n
    yn = d * lax.rsqrt(var + GN_EPS) * gn_g + gn_b
    o_ref[0] = ((yn + bv_ref[0]) * g_ref[0]).astype(o_ref.dtype)


def _post(y, bv, g, vecs, e, layer, tt):
    b, t, c = y.shape
    tt = min(tt, t)
    spec = pl.BlockSpec((1, tt, c), lambda bi, i: (bi, i, 0))
    return pl.pallas_call(
        _post_kernel,
        grid=(b, t // tt),
        in_specs=[spec] * 3 + [
            pl.BlockSpec((None,) + vecs.shape[1:], lambda bi, i: (layer, 0, 0)),
            pl.BlockSpec(e.shape, lambda bi, i: (0, 0)),
        ],
        out_specs=spec,
        out_shape=jax.ShapeDtypeStruct((b, t, c), BF16),
        compiler_params=_params("parallel", "parallel"),
        name="rwkv_post",
    )(y, bv, g, vecs, e)


def kernel(x, p, attn_norm, w_in, mu_shift, w_vres_dn, mu_vres, v0, v_up, pool_w, pool_scale,
           w0, w_up, a0, a_up, g_up, k_k, k_a, r_k, gn_g, gn_b, w_out, mlp_norm, w_ffn_up,
           w_ffn_down, ple_norm, w_ple_gate, w_ple_proj, final_norm):
    bsz, t, d = x.shape
    depth = w_in.shape[0]
    width = w0.shape[1]
    pool_width = pool_scale.shape[1]
    d_w, d_a, d_g, d_v = w_up.shape[1], a_up.shape[1], g_up.shape[1], v_up.shape[1]
    d_ple = p.shape[-1]
    m = bsz * t
    assert pool_width == width and d_w + d_a == LANE and width % SEG_SLAB == 0
    lora_w = 2 * SEG_SLAB
    w2_rows = lora_w - LANE - LANE
    assert d_g + d_v <= w2_rows
    zw = pool_width + 3 * width + lora_w

    n_in = w_in.shape[2]
    vres = jnp.concatenate([jnp.zeros((1, d, d_v), F32), w_vres_dn], axis=0)
    w_in_b = jnp.concatenate(
        [w_in[0].astype(BF16), jnp.zeros((d, zw - n_in), BF16)], axis=1)
    mu_vres_x = jnp.concatenate([jnp.zeros((1, d_v), F32), mu_vres], axis=0)
    mu_x = jnp.concatenate(
        [mu_shift, mu_vres_x, jnp.zeros((depth, zw - n_in - d_v), F32)], axis=1)[:, None, :]
    zeros_w = lambda rows: jnp.zeros((depth, rows, width), F32)
    w1 = jnp.concatenate([
        jnp.concatenate([w_up, zeros_w(d_w)], axis=2),
        jnp.concatenate([zeros_w(d_a), a_up], axis=2)], axis=1).astype(BF16)
    v_up_x = jnp.concatenate([jnp.zeros((1, d_v, width), F32), v_up], axis=0)
    w2 = jnp.concatenate([
        jnp.concatenate([g_up, zeros_w(d_g)], axis=2),
        jnp.concatenate([zeros_w(d_v), v_up_x], axis=2),
        jnp.zeros((depth, w2_rows - d_g - d_v, 2 * width), F32)], axis=1).astype(BF16)
    v0_x = jnp.concatenate([jnp.zeros((1, width), F32), v0], axis=0)
    pad = lambda rows: jnp.zeros((depth, rows, width), F32)
    prep_vecs = jnp.concatenate(
        [jnp.stack([w0, a0, v0_x, k_k, k_a, r_k.reshape(depth, width)], axis=1), pad(2)],
        axis=1)
    post_vecs = jnp.concatenate([jnp.stack([gn_g, gn_b], axis=1), pad(6)], axis=1)
    seg = jnp.arange(SEG_SLAB) // HEAD_SIZE
    e = (seg[:, None] == seg[None, :]).astype(BF16)
    attn_g, mlp_g, ple_g = attn_norm[:, None, :], mlp_norm[:, None, :], ple_norm[:, None, :]
    pool_wb = pool_w.astype(BF16)
    pool_sc = pool_scale[:, None, :]
    w_proj_b = w_ple_proj.astype(BF16)
    p2 = p.reshape(depth * m, d_ple)
    final_g = final_norm[None, :]

    xf = x.reshape(m, d)
    v_first = None
    for i in range(depth):
        z2 = _rms_matmul(xf, attn_g, w_in_b, i, tm=512, tn=768)
        z = z2.reshape(bsz, t, zw)
        first = i == 0
        at, rt, bt, kt, vb, wl, bv, g, *v_new = _prep(
            z, mu_x, prep_vecs, w1, w2, e, z if first else v_first, i, first, width, d_w, d_g, tt=256)
        if first:
            v_first = v_new[0]
        last = i == depth - 1
        y, cast = _wkv(at, rt, bt, kt, vb, wl, (w_out, w_ffn_up, w_ffn_down, w_ple_gate), i,
                       join=() if last else (w_in, vres), join_layer=i + 1, join_width=zw)
        w_out_b, w_up_b, w_down_b, w_gate_b = cast[:4]
        if not last:
            w_in_b = cast[4]
        rwkv_out = _post(y, bv, g, post_vecs, e, i, tt=512)
        xf = _outproj(xf, z2, rwkv_out.reshape(m, width), pool_wb, pool_sc, w_out_b, i, t,
                      tm=512, tn=512)
        xf = _ffn(xf, mlp_g, w_up_b, w_down_b, i, tm=512, tf=1024)
        xf = _ple(xf, ple_g, w_gate_b, p2, w_proj_b, final_g, i, i == depth - 1, tm=512)
    return xf.reshape(bsz, t, d)
```
